```python
import jax, jax.numpy as jnp
from jax import lax
import numpy as np

D_MODEL = 2048
BATCH = 4
SEQ = 2048
DEPTH = 1

D_MIX = D_MODEL
CHUNK = 128
SGU_GROUPS = 8
SGU_DIM = (D_MIX // 2) // SGU_GROUPS
SGU_WIDTH = SGU_GROUPS * SGU_DIM
HEAD_DIM = 64
N_Q_HEADS = (D_MIX // 2) // HEAD_DIM
N_KV_HEADS = 2
WINDOW = 128
ATTN_WIDTH = N_Q_HEADS * HEAD_DIM
KV_WIDTH = N_KV_HEADS * HEAD_DIM
IN_WIDTH = 2 * SGU_WIDTH + ATTN_WIDTH + 2 * KV_WIDTH
PEER_HEADS = 8
N_KEYS = 128
N_EXPERTS = N_KEYS * N_KEYS
PEER_TOPK = 16
D_QUERY = 256
EXPERT_BLOCK = 128
EPS = 1e-6

kernel_name = "hybrid_gmlp_swa_sink_peer_block"


def rmsnorm(x, g):
    xf = x.astype(jnp.float32)
    y = xf * lax.rsqrt(jnp.mean(xf * xf, axis=-1, keepdims=True) + EPS)
    return (y * g.astype(jnp.float32)).astype(x.dtype)


def chunked_spatial_gating(u, v, ln_g, ln_b, w_s, b_s):
    B, S, _ = u.shape
    nc = S // CHUNK
    u = u.reshape(B, nc, CHUNK, SGU_GROUPS, SGU_DIM)
    v = v.reshape(B, nc, CHUNK, SGU_GROUPS, SGU_DIM)
    vf = v.astype(jnp.float32)
    mu = jnp.mean(vf, axis=-1, keepdims=True)
    var = jnp.mean(jnp.square(vf - mu), axis=-1, keepdims=True)
    vn = ((vf - mu) * lax.rsqrt(var + EPS) * ln_g.astype(jnp.float32) + ln_b.astype(jnp.float32)).astype(u.dtype)
    causal = jnp.tril(jnp.ones((CHUNK, CHUNK), dtype=bool))
    w = jnp.where(causal[None], w_s, jnp.zeros_like(w_s))
    s = jnp.einsum('gts,bcsgd->bctgd', w, vn) + b_s.T[None, None, :, :, None]
    return (u * s).reshape(B, S, SGU_WIDTH)


def sliding_window_attention(q, k, v, sinks):
    B, S = q.shape[0], q.shape[1]
    nb = S // WINDOW
    G = N_Q_HEADS // N_KV_HEADS
    qb = q.reshape(B, nb, WINDOW, N_KV_HEADS, G, HEAD_DIM)
    kb = k.reshape(B, nb, WINDOW, N_KV_HEADS, HEAD_DIM)
    vb = v.reshape(B, nb, WINDOW, N_KV_HEADS, HEAD_DIM)
    pad = ((0, 0), (1, 0), (0, 0), (0, 0), (0, 0))
    kk = jnp.concatenate([jnp.pad(kb[:, :-1], pad), kb], axis=2)
    vv = jnp.concatenate([jnp.pad(vb[:, :-1], pad), vb], axis=2)
    scores = jnp.einsum('bnqhgd,bnkhd->bnhgqk', qb, kk).astype(jnp.float32) * (HEAD_DIM ** -0.5)
    i = jnp.arange(WINDOW)[:, None]
    j = jnp.arange(2 * WINDOW)[None, :]
    band = (j >= i + 1) & (j <= i + WINDOW)
    blk = jnp.arange(nb)[:, None, None]
    mask = band[None] & ((blk > 0) | (j >= WINDOW)[None])
    scores = jnp.where(mask[None, :, None, None], scores, -jnp.inf)
    sink = sinks.astype(jnp.float32).reshape(N_KV_HEADS, G)[None, None, :, :, None, None]
    m = jnp.maximum(jnp.max(scores, axis=-1, keepdims=True), sink)
    p = jnp.exp(scores - m)
    denom = jnp.sum(p, axis=-1, keepdims=True) + jnp.exp(sink - m)
    probs = (p / denom).astype(v.dtype)
    out = jnp.einsum('bnhgqk,bnkhd->bnqhgd', probs, vv)
    return out.reshape(B, S, ATTN_WIDTH)


def peer(h, w_query, sub_keys, expert_down, expert_up):
    B, S, D = h.shape
    T = B * S
    xf = h.reshape(T, D)
    q = (xf @ w_query).reshape(T, PEER_HEADS, 2, D_QUERY // 2)
    s1 = jnp.einsum('thd,kd->thk', q[:, :, 0], sub_keys[0]).astype(jnp.float32)
    s2 = jnp.einsum('thd,kd->thk', q[:, :, 1], sub_keys[1]).astype(jnp.float32)
    v1, i1 = lax.top_k(s1, PEER_TOPK)
    v2, i2 = lax.top_k(s2, PEER_TOPK)
    cand = (v1[..., :, None] + v2[..., None, :]).reshape(T, PEER_HEADS, PEER_TOPK * PEER_TOPK)
    top_v, flat = lax.top_k(cand, PEER_TOPK)
    e1 = jnp.take_along_axis(i1, flat // PEER_TOPK, axis=-1)
    e2 = jnp.take_along_axis(i2, flat % PEER_TOPK, axis=-1)
    experts = e1 * N_KEYS + e2
    gates = jax.nn.softmax(top_v, axis=-1).astype(h.dtype)
    nblk = T // EXPERT_BLOCK

    def block(args):
        xc, ec, gc = args
        u = jnp.take(expert_down, ec, axis=0)
        a = jax.nn.gelu(jnp.einsum('cd,chkd->chk', xc, u), approximate=False)
        vt = jnp.take(expert_up, ec, axis=0)
        return jnp.einsum('chk,chkd->cd', gc * a, vt)

    out = lax.map(block, (xf.reshape(nblk, EXPERT_BLOCK, D),
                          experts.reshape(nblk, EXPERT_BLOCK, PEER_HEADS, PEER_TOPK),
                          gates.reshape(nblk, EXPERT_BLOCK, PEER_HEADS, PEER_TOPK)))
    return out.reshape(B, S, D)


def setup_inputs(seed: int = 0) -> dict:
    key = jax.random.key(seed)
    ks = jax.random.split(key, 16)
    L = DEPTH

    def nrm(k, shape, scale):
        return jax.random.normal(k, shape, jnp.float32) * scale

    return {
        "x": nrm(ks[0], (BATCH, SEQ, D_MODEL), 1.0),
        "norm1_g": 1.0 + nrm(ks[1], (L, D_MODEL), 0.01),
        "w_in": nrm(ks[2], (L, D_MODEL, IN_WIDTH), D_MODEL ** -0.5),
        "sgu_ln_g": 1.0 + nrm(ks[3], (L, SGU_GROUPS, SGU_DIM), 0.01),
        "sgu_ln_b": nrm(ks[4], (L, SGU_GROUPS, SGU_DIM), 0.01),
        "w_spatial": nrm(ks[5], (L, SGU_GROUPS, CHUNK, CHUNK), CHUNK ** -0.5),
        "b_spatial": 1.0 + nrm(ks[6], (L, SGU_GROUPS, CHUNK), 0.01),
        "attn_sinks": nrm(ks[7], (L, N_Q_HEADS), 0.5),
        "w_out": nrm(ks[8], (L, D_MIX, D_MODEL), D_MIX ** -0.5),
        "norm2_g": 1.0 + nrm(ks[9], (L, D_MODEL), 0.01),
        "w_query": nrm(ks[10], (L, D_MODEL, PEER_HEADS * D_QUERY), D_MODEL ** -0.5),
        "sub_keys": nrm(ks[11], (L, 2, N_KEYS, D_QUERY // 2), (D_QUERY // 2) ** -0.5),
        "expert_down": nrm(ks[12], (L, N_EXPERTS, D_MODEL), D_MODEL ** -0.5),
        "expert_up": nrm(ks[13], (L, N_EXPERTS, D_MODEL), PEER_HEADS ** -0.5),
        "norm_f_g": 1.0 + nrm(ks[14], (D_MODEL,), 0.01),
    }


def reference(x, norm1_g, w_in, sgu_ln_g, sgu_ln_b, w_spatial, b_spatial, attn_sinks,
              w_out, norm2_g, w_query, sub_keys, expert_down, expert_up, norm_f_g):
    B, S, _ = x.shape
    splits = [SGU_WIDTH, 2 * SGU_WIDTH, 2 * SGU_WIDTH + ATTN_WIDTH,
              2 * SGU_WIDTH + ATTN_WIDTH + KV_WIDTH]
    for l in range(DEPTH):
        h = rmsnorm(x, norm1_g[l])
        z = h @ w_in[l]
        zu, zv, zq, zk, zvv = jnp.split(z, splits, axis=-1)
        a_out = chunked_spatial_gating(jax.nn.gelu(zu, approximate=False),
                                       jax.nn.gelu(zv, approximate=False),
                                       sgu_ln_g[l], sgu_ln_b[l], w_spatial[l], b_spatial[l])
        b_out = sliding_window_attention(zq.reshape(B, S, N_Q_HEADS, HEAD_DIM),
                                         zk.reshape(B, S, N_KV_HEADS, HEAD_DIM),
                                         zvv.reshape(B, S, N_KV_HEADS, HEAD_DIM),
                                         attn_sinks[l])
        x = x + jnp.concatenate([a_out, b_out], axis=-1) @ w_out[l]
        h2 = rmsnorm(x, norm2_g[l])
        x = x + peer(h2, w_query[l], sub_keys[l], expert_down[l], expert_up[l])
    return rmsnorm(x, norm_f_g)
```

```python
import functools
import math

import jax
import jax.numpy as jnp
from jax import lax
from jax.experimental import pallas as pl
from jax.experimental.pallas import tpu as pltpu

D_MODEL = 2048
CHUNK = 128
SGU_GROUPS = 8
SGU_DIM = 128
SGU_WIDTH = SGU_GROUPS * SGU_DIM
HEAD_DIM = 64
N_Q_HEADS = 16
N_KV_HEADS = 2
Q_PER_KV = N_Q_HEADS // N_KV_HEADS
WINDOW = 128
ATTN_WIDTH = N_Q_HEADS * HEAD_DIM
KV_WIDTH = N_KV_HEADS * HEAD_DIM
IN_WIDTH = 2 * SGU_WIDTH + ATTN_WIDTH + 2 * KV_WIDTH
PEER_HEADS = 8
N_KEYS = 128
N_EXPERTS = N_KEYS * N_KEYS
PEER_TOPK = 16
D_QUERY = 256
EPS = 1e-6

VMEM_LIMIT_BYTES = 56 * 1024 * 1024

BF16 = jnp.bfloat16
F32 = jnp.float32


def _gelu(x):
    return 0.5 * x * (1.0 + lax.erf(x * np_sqrt_half))


np_sqrt_half = math.sqrt(0.5)


def _rmsnorm(x, g):
    return x * lax.rsqrt(jnp.mean(x * x, axis=-1, keepdims=True) + EPS) * g


def _resident(shape):
    nd = len(shape)
    return pl.BlockSpec(shape, lambda *_: (0,) * nd, pipeline_mode=pl.Buffered(1))


def _in_proj_kernel(x_ref, g_ref, w_ref, u_ref, v_ref, q_ref, kv_ref):
    h = _rmsnorm(x_ref[...], g_ref[...]).astype(BF16)
    o0, o1, o2, o3 = SGU_WIDTH, 2 * SGU_WIDTH, 2 * SGU_WIDTH + ATTN_WIDTH, IN_WIDTH
    zu = jnp.dot(h, w_ref[:, 0:o0], preferred_element_type=F32)
    u_ref[...] = _gelu(zu).astype(BF16)
    zv = jnp.dot(h, w_ref[:, o0:o1], preferred_element_type=F32)
    v_ref[...] = _gelu(zv).astype(BF16)
    q_ref[...] = jnp.dot(h, w_ref[:, o1:o2], preferred_element_type=F32).astype(BF16)
    kv_ref[...] = jnp.dot(h, w_ref[:, o2:o3], preferred_element_type=F32).astype(BF16)


def _in_proj(x2, g1, w_in_bf, tm):
    T = x2.shape[0]
    row = lambda w: pl.BlockSpec((tm, w), lambda i: (i, 0))
    return pl.pallas_call(
        _in_proj_kernel,
        grid=(T // tm,),
        in_specs=[row(D_MODEL), _resident((1, D_MODEL)), _resident((D_MODEL, IN_WIDTH))],
        out_specs=[row(SGU_WIDTH), row(SGU_WIDTH), row(ATTN_WIDTH), row(2 * KV_WIDTH)],
        out_shape=[
            jax.ShapeDtypeStruct((T, SGU_WIDTH), BF16),
            jax.ShapeDtypeStruct((T, SGU_WIDTH), BF16),
            jax.ShapeDtypeStruct((T, ATTN_WIDTH), BF16),
            jax.ShapeDtypeStruct((T, 2 * KV_WIDTH), BF16),
        ],
        compiler_params=pltpu.CompilerParams(
            dimension_semantics=("arbitrary",), vmem_limit_bytes=VMEM_LIMIT_BYTES),
        name="in_proj",
    )(x2, g1, w_in_bf)


def _mix_kernel(sink_ref, u_ref, v_ref, q_ref, kv_ref, kvp_ref, lng_ref, lnb_ref,
                ws_ref, bs_ref, y_ref):
    n = pl.program_id(1)
    row_t = lax.broadcasted_iota(jnp.int32, (CHUNK, CHUNK), 0)
    col_s = lax.broadcasted_iota(jnp.int32, (CHUNK, CHUNK), 1)
    causal = col_s <= row_t
    for g in range(SGU_GROUPS):
        sl = slice(g * SGU_DIM, (g + 1) * SGU_DIM)
        vg = v_ref[:, sl].astype(F32)
        mu = jnp.mean(vg, axis=-1, keepdims=True)
        var = jnp.mean(jnp.square(vg - mu), axis=-1, keepdims=True)
        vn = ((vg - mu) * lax.rsqrt(var + EPS) * lng_ref[:, sl] + lnb_ref[:, sl]).astype(BF16)
        w = jnp.where(causal, ws_ref[g], 0.0).astype(BF16)
        s = jnp.dot(w, vn, preferred_element_type=F32) + bs_ref[:, sl]
        y_ref[:, sl] = (u_ref[:, sl].astype(F32) * s).astype(BF16)

    qi = lax.broadcasted_iota(jnp.int32, (WINDOW, 2 * WINDOW), 0)
    kj = lax.broadcasted_iota(jnp.int32, (WINDOW, 2 * WINDOW), 1)
    band = (kj >= qi + 1) & (kj <= qi + WINDOW)
    mask = band & ((n > 0) | (kj >= WINDOW))
    scale = HEAD_DIM ** -0.5
    for j in range(N_KV_HEADS):
        ksl = slice(j * HEAD_DIM, (j + 1) * HEAD_DIM)
        vsl = slice(KV_WIDTH + j * HEAD_DIM, KV_WIDTH + (j + 1) * HEAD_DIM)
        kk = jnp.concatenate([kvp_ref[:, ksl], kv_ref[:, ksl]], axis=0)
        vv = jnp.concatenate([kvp_ref[:, vsl], kv_ref[:, vsl]], axis=0)
        for gq in range(Q_PER_KV):
            hq = j * Q_PER_KV + gq
            qh = q_ref[:, hq * HEAD_DIM:(hq + 1) * HEAD_DIM]
            sc = lax.dot_general(qh, kk, (((1,), (1,)), ((), ())),
                                 preferred_element_type=F32) * scale
            sc = jnp.where(mask, sc, -jnp.inf)
            sink = sink_ref[hq]
            m = jnp.maximum(jnp.max(sc, axis=-1, keepdims=True), sink)
            p = jnp.exp(sc - m)
            denom = jnp.sum(p, axis=-1, keepdims=True) + jnp.exp(sink - m)
            probs = (p / denom).astype(BF16)
            oh = jnp.dot(probs, vv, preferred_element_type=F32)
            c0 = SGU_WIDTH + hq * HEAD_DIM
            y_ref[:, c0:c0 + HEAD_DIM] = oh.astype(BF16)


def _mix(sinks, u, v, q, kv, lng, lnb, ws, bs_full, batch, nb):
    T = u.shape[0]
    cur = lambda w: pl.BlockSpec((WINDOW, w), lambda b, n: (b * nb + n, 0))
    prev = pl.BlockSpec((WINDOW, 2 * KV_WIDTH),
                        lambda b, n: (b * nb + jnp.maximum(n - 1, 0), 0))
    return pl.pallas_call(
        _mix_kernel,
        grid=(batch, nb),
        in_specs=[
            pl.BlockSpec(memory_space=pltpu.SMEM),
            cur(SGU_WIDTH), cur(SGU_WIDTH), cur(ATTN_WIDTH), cur(2 * KV_WIDTH), prev,
            _resident((1, SGU_WIDTH)), _resident((1, SGU_WIDTH)),
            _resident((SGU_GROUPS, CHUNK, CHUNK)), _resident((CHUNK, SGU_WIDTH)),
        ],
        out_specs=cur(D_MODEL),
        out_shape=jax.ShapeDtypeStruct((T, D_MODEL), BF16),
        compiler_params=pltpu.CompilerParams(
            dimension_semantics=("arbitrary", "arbitrary"), vmem_limit_bytes=VMEM_LIMIT_BYTES),
        name="mix",
    )(sinks, u, v, q, kv, kv, lng, lnb, ws, bs_full)


def _out_proj_kernel(y_ref, x_ref, wo_ref, g2_ref, wq_ref, sk_ref,
                     x1_ref, h2_ref, s1_ref, s2_ref):
    x1 = x_ref[...] + jnp.dot(y_ref[...], wo_ref[...], preferred_element_type=F32)
    x1_ref[...] = x1
    h2 = _rmsnorm(x1, g2_ref[...]).astype(BF16)
    h2_ref[...] = h2
    qry = jnp.dot(h2, wq_ref[...], preferred_element_type=F32).astype(BF16)
    half = D_QUERY // 2
    for h in range(PEER_HEADS):
        for c, s_ref in ((0, s1_ref), (1, s2_ref)):
            qc = qry[:, h * D_QUERY + c * half: h * D_QUERY + (c + 1) * half]
            s_ref[h] = lax.dot_general(sk_ref[c], qc, (((1,), (1,)), ((), ())),
                                       preferred_element_type=F32)


def _out_proj(y, x2, w_out_bf, g2, w_query_bf, sub_keys_bf, tm):
    T = x2.shape[0]
    row = lambda w: pl.BlockSpec((tm, w), lambda i: (i, 0))
    st = pl.BlockSpec((PEER_HEADS, N_KEYS, tm), lambda i: (0, 0, i))
    return pl.pallas_call(
        _out_proj_kernel,
        grid=(T // tm,),
        in_specs=[row(D_MODEL), row(D_MODEL), _resident((D_MODEL, D_MODEL)),
                  _resident((1, D_MODEL)), _resident((D_MODEL, PEER_HEADS * D_QUERY)),
                  _resident((2, N_KEYS, D_QUERY // 2))],
        out_specs=[row(D_MODEL), row(D_MODEL), st, st],
        out_shape=[
            jax.ShapeDtypeStruct((T, D_MODEL), F32),
            jax.ShapeDtypeStruct((T, D_MODEL), BF16),
            jax.ShapeDtypeStruct((PEER_HEADS, N_KEYS, T), F32),
            jax.ShapeDtypeStruct((PEER_HEADS, N_KEYS, T), F32),
        ],
        compiler_params=pltpu.CompilerParams(
            dimension_semantics=("arbitrary",), vmem_limit_bytes=VMEM_LIMIT_BYTES),
        name="out_proj",
    )(y, x2, w_out_bf, g2, w_query_bf, sub_keys_bf)


def _top_desc(s, k):
    out = []
    for _ in range(k):
        mx = jnp.max(s, axis=0, keepdims=True)
        out.append(mx)
        s = jnp.where(s == mx, -jnp.inf, s)
    return out


def _route_kernel(s1_ref, s2_ref, p1_ref, p2_ref, tau_ref):
    for h in range(PEER_HEADS):
        s1 = s1_ref[h]
        s2 = s2_ref[h]
        v1 = _top_desc(s1, PEER_TOPK)
        v2 = _top_desc(s2, PEER_TOPK)
        cand = jnp.concatenate(
            [v1[i] + v2[j] for i in range(PEER_TOPK) for j in range(PEER_TOPK)
             if (i + 1) * (j + 1) <= PEER_TOPK], axis=0)
        tau = _top_desc(cand, PEER_TOPK)[-1]
        m1, m2 = v1[0], v2[0]
        z = jnp.sum(jnp.where(cand >= tau, jnp.exp(cand - (m1 + m2)), 0.0),
                    axis=0, keepdims=True)
        p1_ref[h] = jnp.exp(s1 - m1) / z
        p2_ref[h] = jnp.exp(s2 - m2)
        tau_ref[pl.ds(h, 1), :] = tau


def _route(s1t, s2t, tr):
    T = s1t.shape[-1]
    st = pl.BlockSpec((PEER_HEADS, N_KEYS, tr), lambda i: (0, 0, i))
    return pl.pallas_call(
        _route_kernel,
        grid=(T // tr,),
        in_specs=[st, st],
        out_specs=[st, st, pl.BlockSpec((PEER_HEADS, tr), lambda i: (0, i))],
        out_shape=[
            jax.ShapeDtypeStruct((PEER_HEADS, N_KEYS, T), F32),
            jax.ShapeDtypeStruct((PEER_HEADS, N_KEYS, T), F32),
            jax.ShapeDtypeStruct((PEER_HEADS, T), F32),
        ],
        compiler_params=pltpu.CompilerParams(
            dimension_semantics=("arbitrary",), vmem_limit_bytes=VMEM_LIMIT_BYTES),
        name="route",
    )(s1t, s2t)


def _peer_kernel(h2_ref, dn_ref, upt_ref, s1_ref, p1_ref, s2_ref, p2_ref, tau_ref,
                 o_ref, *, rows_e1):
    j = pl.program_id(1)

    @pl.when(j == 0)
    def _():
        o_ref[...] = jnp.zeros_like(o_ref)

    a = lax.dot_general(dn_ref[...], h2_ref[...], (((1,), (1,)), ((), ())),
                        preferred_element_type=F32)
    act = _gelu(a)
    tm = a.shape[1]
    w_rows = []
    for r in range(rows_e1):
        w = jnp.zeros((N_KEYS, tm), F32)
        for h in range(PEER_HEADS):
            s1r = s1_ref[h, pl.ds(j * rows_e1 + r, 1), :]
            p1r = p1_ref[h, pl.ds(j * rows_e1 + r, 1), :]
            sel = (s1r + s2_ref[h]) >= tau_ref[pl.ds(h, 1), :]
            w = w + jnp.where(sel, p1r * p2_ref[h], 0.0)
        w_rows.append(w)
    wt = jnp.concatenate(w_rows, axis=0)
    c = (wt * act).astype(BF16)
    o_ref[...] += jnp.dot(upt_ref[...], c, preferred_element_type=F32)


def _peer(h2, dn_bf, upt_bf, s1t, p1t, s2t, p2t, tau, tm, rows_e1):
    T = h2.shape[0]
    E = rows_e1 * N_KEYS
    return pl.pallas_call(
        functools.partial(_peer_kernel, rows_e1=rows_e1),
        grid=(T // tm, N_EXPERTS // E),
        in_specs=[
            pl.BlockSpec((tm, D_MODEL), lambda i, j: (i, 0)),
            pl.BlockSpec((E, D_MODEL), lambda i, j: (j, 0)),
            pl.BlockSpec((D_MODEL, E), lambda i, j: (0, j)),
            pl.BlockSpec((PEER_HEADS, N_KEYS, tm), lambda i, j: (0, 0, i)),
            pl.BlockSpec((PEER_HEADS, N_KEYS, tm), lambda i, j: (0, 0, i)),
            pl.BlockSpec((PEER_HEADS, N_KEYS, tm), lambda i, j: (0, 0, i)),
            pl.BlockSpec((PEER_HEADS, N_KEYS, tm), lambda i, j: (0, 0, i)),
            pl.BlockSpec((PEER_HEADS, tm), lambda i, j: (0, i)),
        ],
        out_specs=pl.BlockSpec((D_MODEL, tm), lambda i, j: (0, i)),
        out_shape=jax.ShapeDtypeStruct((D_MODEL, T), F32),
        compiler_params=pltpu.CompilerParams(
            dimension_semantics=("arbitrary", "arbitrary"), vmem_limit_bytes=VMEM_LIMIT_BYTES),
        name="peer",
    )(h2, dn_bf, upt_bf, s1t, p1t, s2t, p2t, tau)


def _final_kernel(x1_ref, pt_ref, g_ref, o_ref):
    xo = x1_ref[...] + pt_ref[...].T
    o_ref[...] = _rmsnorm(xo, g_ref[...])


def _final(x1, peer_t, gf, tm):
    T = x1.shape[0]
    return pl.pallas_call(
        _final_kernel,
        grid=(T // tm,),
        in_specs=[pl.BlockSpec((tm, D_MODEL), lambda i: (i, 0)),
                  pl.BlockSpec((D_MODEL, tm), lambda i: (0, i)),
                  _resident((1, D_MODEL))],
        out_specs=pl.BlockSpec((tm, D_MODEL), lambda i: (i, 0)),
        out_shape=jax.ShapeDtypeStruct((T, D_MODEL), F32),
        compiler_params=pltpu.CompilerParams(
            dimension_semantics=("arbitrary",), vmem_limit_bytes=VMEM_LIMIT_BYTES),
        name="final",
    )(x1, peer_t, gf)


def kernel(x, norm1_g, w_in, sgu_ln_g, sgu_ln_b, w_spatial, b_spatial, attn_sinks, w_out,
           norm2_g, w_query, sub_keys, expert_down, expert_up, norm_f_g):
    B, S, D = x.shape
    assert D == D_MODEL and S % WINDOW == 0 and norm1_g.shape[0] == 1
    T = B * S
    nb = S // WINDOW
    x2 = x.reshape(T, D)

    u, v, q, kv = _in_proj(x2, norm1_g[0].reshape(1, D), w_in[0].astype(BF16), tm=512)

    bs_full = jnp.repeat(b_spatial[0].T, SGU_DIM, axis=1)
    y = _mix(attn_sinks[0], u, v, q, kv,
             sgu_ln_g[0].reshape(1, SGU_WIDTH), sgu_ln_b[0].reshape(1, SGU_WIDTH),
             w_spatial[0], bs_full, B, nb)

    x1, h2, s1t, s2t = _out_proj(y, x2, w_out[0].astype(BF16), norm2_g[0].reshape(1, D),
                                 w_query[0].astype(BF16), sub_keys[0].astype(BF16), tm=256)

    p1t, p2t, tau = _route(s1t, s2t, tr=256)

    peer_t = _peer(h2, expert_down[0].astype(BF16), expert_up[0].T.astype(BF16),
                   s1t, p1t, s2t, p2t, tau, tm=512, rows_e1=4)

    out = _final(x1, peer_t, norm_f_g.reshape(1, D), tm=256)
    return out.reshape(B, S, D)
```

```python
import functools
import math

import jax
import jax.numpy as jnp
from jax import lax
from jax.experimental import pallas as pl
from jax.experimental.pallas import tpu as pltpu

D_MODEL = 2048
CHUNK = 128
SGU_GROUPS = 8
SGU_DIM = 128
SGU_WIDTH = SGU_GROUPS * SGU_DIM
HEAD_DIM = 64
N_Q_HEADS = 16
N_KV_HEADS = 2
Q_PER_KV = N_Q_HEADS // N_KV_HEADS
WINDOW = 128
ATTN_WIDTH = N_Q_HEADS * HEAD_DIM
KV_WIDTH = N_KV_HEADS * HEAD_DIM
IN_WIDTH = 2 * SGU_WIDTH + ATTN_WIDTH + 2 * KV_WIDTH
PEER_HEADS = 8
N_KEYS = 128
N_EXPERTS = N_KEYS * N_KEYS
PEER_TOPK = 16
D_QUERY = 256
EPS = 1e-6

VMEM_LIMIT_BYTES = 56 * 1024 * 1024

BF16 = jnp.bfloat16
F32 = jnp.float32


def _gelu(x):
    return 0.5 * x * (1.0 + lax.erf(x * np_sqrt_half))


np_sqrt_half = math.sqrt(0.5)


def _rmsnorm(x, g):
    return x * lax.rsqrt(jnp.mean(x * x, axis=-1, keepdims=True) + EPS) * g


def _resident(shape):
    nd = len(shape)
    return pl.BlockSpec(shape, lambda *_: (0,) * nd, pipeline_mode=pl.Buffered(1))


def _in_proj_kernel(x_ref, g_ref, w_ref, u_ref, v_ref, q_ref, kv_ref):
    h = _rmsnorm(x_ref[...], g_ref[...]).astype(BF16)
    o0, o1, o2, o3 = SGU_WIDTH, 2 * SGU_WIDTH, 2 * SGU_WIDTH + ATTN_WIDTH, IN_WIDTH
    zu = jnp.dot(h, w_ref[:, 0:o0], preferred_element_type=F32)
    u_ref[...] = _gelu(zu).astype(BF16)
    zv = jnp.dot(h, w_ref[:, o0:o1], preferred_element_type=F32)
    v_ref[...] = _gelu(zv).astype(BF16)
    q_ref[...] = jnp.dot(h, w_ref[:, o1:o2], preferred_element_type=F32).astype(BF16)
    kv_ref[...] = jnp.dot(h, w_ref[:, o2:o3], preferred_element_type=F32).astype(BF16)


def _in_proj(x2, g1, w_in_bf, tm):
    T = x2.shape[0]
    row = lambda w: pl.BlockSpec((tm, w), lambda i: (i, 0))
    return pl.pallas_call(
        _in_proj_kernel,
        grid=(T // tm,),
        in_specs=[row(D_MODEL), _resident((1, D_MODEL)), _resident((D_MODEL, IN_WIDTH))],
        out_specs=[row(SGU_WIDTH), row(SGU_WIDTH), row(ATTN_WIDTH), row(2 * KV_WIDTH)],
        out_shape=[
            jax.ShapeDtypeStruct((T, SGU_WIDTH), BF16),
            jax.ShapeDtypeStruct((T, SGU_WIDTH), BF16),
            jax.ShapeDtypeStruct((T, ATTN_WIDTH), BF16),
            jax.ShapeDtypeStruct((T, 2 * KV_WIDTH), BF16),
        ],
        compiler_params=pltpu.CompilerParams(
            dimension_semantics=("arbitrary",), vmem_limit_bytes=VMEM_LIMIT_BYTES),
        name="in_proj",
    )(x2, g1, w_in_bf)


def _mix_kernel(sink_ref, u_ref, v_ref, q_ref, kv_ref, kvp_ref, lng_ref, lnb_ref,
                ws_ref, bs_ref, y_ref):
    n = pl.program_id(1)
    row_t = lax.broadcasted_iota(jnp.int32, (CHUNK, CHUNK), 0)
    col_s = lax.broadcasted_iota(jnp.int32, (CHUNK, CHUNK), 1)
    causal = col_s <= row_t
    for g in range(SGU_GROUPS):
        sl = slice(g * SGU_DIM, (g + 1) * SGU_DIM)
        vg = v_ref[:, sl].astype(F32)
        mu = jnp.mean(vg, axis=-1, keepdims=True)
        var = jnp.mean(jnp.square(vg - mu), axis=-1, keepdims=True)
        vn = ((vg - mu) * lax.rsqrt(var + EPS) * lng_ref[:, sl] + lnb_ref[:, sl]).astype(BF16)
        w = jnp.where(causal, ws_ref[g], 0.0).astype(BF16)
        s = jnp.dot(w, vn, preferred_element_type=F32) + bs_ref[:, sl]
        y_ref[:, sl] = (u_ref[:, sl].astype(F32) * s).astype(BF16)

    qi = lax.broadcasted_iota(jnp.int32, (WINDOW, 2 * WINDOW), 0)
    kj = lax.broadcasted_iota(jnp.int32, (WINDOW, 2 * WINDOW), 1)
    band = (kj >= qi + 1) & (kj <= qi + WINDOW)
    mask = band & ((n > 0) | (kj >= WINDOW))
    scale = HEAD_DIM ** -0.5
    for j in range(N_KV_HEADS):
        ksl = slice(j * HEAD_DIM, (j + 1) * HEAD_DIM)
        vsl = slice(KV_WIDTH + j * HEAD_DIM, KV_WIDTH + (j + 1) * HEAD_DIM)
        kk = jnp.concatenate([kvp_ref[:, ksl], kv_ref[:, ksl]], axis=0)
        vv = jnp.concatenate([kvp_ref[:, vsl], kv_ref[:, vsl]], axis=0)
        for gq in range(Q_PER_KV):
            hq = j * Q_PER_KV + gq
            qh = q_ref[:, hq * HEAD_DIM:(hq + 1) * HEAD_DIM]
            sc = lax.dot_general(qh, kk, (((1,), (1,)), ((), ())),
                                 preferred_element_type=F32) * scale
            sc = jnp.where(mask, sc, -jnp.inf)
            sink = sink_ref[hq]
            m = jnp.maximum(jnp.max(sc, axis=-1, keepdims=True), sink)
            p = jnp.exp(sc - m)
            denom = jnp.sum(p, axis=-1, keepdims=True) + jnp.exp(sink - m)
            probs = (p / denom).astype(BF16)
            oh = jnp.dot(probs, vv, preferred_element_type=F32)
            c0 = SGU_WIDTH + hq * HEAD_DIM
            y_ref[:, c0:c0 + HEAD_DIM] = oh.astype(BF16)


def _mix(sinks, u, v, q, kv, lng, lnb, ws, bs_full, batch, nb):
    T = u.shape[0]
    cur = lambda w: pl.BlockSpec((WINDOW, w), lambda b, n: (b * nb + n, 0))
    prev = pl.BlockSpec((WINDOW, 2 * KV_WIDTH),
                        lambda b, n: (b * nb + jnp.maximum(n - 1, 0), 0))
    return pl.pallas_call(
        _mix_kernel,
        grid=(batch, nb),
        in_specs=[
            pl.BlockSpec(memory_space=pltpu.SMEM),
            cur(SGU_WIDTH), cur(SGU_WIDTH), cur(ATTN_WIDTH), cur(2 * KV_WIDTH), prev,
            _resident((1, SGU_WIDTH)), _resident((1, SGU_WIDTH)),
            _resident((SGU_GROUPS, CHUNK, CHUNK)), _resident((CHUNK, SGU_WIDTH)),
        ],
        out_specs=cur(D_MODEL),
        out_shape=jax.ShapeDtypeStruct((T, D_MODEL), BF16),
        compiler_params=pltpu.CompilerParams(
            dimension_semantics=("arbitrary", "arbitrary"), vmem_limit_bytes=VMEM_LIMIT_BYTES),
        name="mix",
    )(sinks, u, v, q, kv, kv, lng, lnb, ws, bs_full)


def _out_proj_kernel(y_ref, x_ref, wo_ref, g2_ref, wq_ref, sk_ref,
                     x1_ref, h2_ref, s1_ref, s2_ref):
    x1 = x_ref[...] + jnp.dot(y_ref[...], wo_ref[...], preferred_element_type=F32)
    x1_ref[...] = x1
    h2 = _rmsnorm(x1, g2_ref[...]).astype(BF16)
    h2_ref[...] = h2
    qry = jnp.dot(h2, wq_ref[...], preferred_element_type=F32).astype(BF16)
    half = D_QUERY // 2
    for h in range(PEER_HEADS):
        for c, s_ref in ((0, s1_ref), (1, s2_ref)):
            qc = qry[:, h * D_QUERY + c * half: h * D_QUERY + (c + 1) * half]
            s_ref[h] = lax.dot_general(sk_ref[c], qc, (((1,), (1,)), ((), ())),
                                       preferred_element_type=F32)


def _out_proj(y, x2, w_out_bf, g2, w_query_bf, sub_keys_bf, tm):
    T = x2.shape[0]
    row = lambda w: pl.BlockSpec((tm, w), lambda i: (i, 0))
    st = pl.BlockSpec((PEER_HEADS, N_KEYS, tm), lambda i: (0, 0, i))
    return pl.pallas_call(
        _out_proj_kernel,
        grid=(T // tm,),
        in_specs=[row(D_MODEL), row(D_MODEL), _resident((D_MODEL, D_MODEL)),
                  _resident((1, D_MODEL)), _resident((D_MODEL, PEER_HEADS * D_QUERY)),
                  _resident((2, N_KEYS, D_QUERY // 2))],
        out_specs=[row(D_MODEL), row(D_MODEL), st, st],
        out_shape=[
            jax.ShapeDtypeStruct((T, D_MODEL), F32),
            jax.ShapeDtypeStruct((T, D_MODEL), BF16),
            jax.ShapeDtypeStruct((PEER_HEADS, N_KEYS, T), F32),
            jax.ShapeDtypeStruct((PEER_HEADS, N_KEYS, T), F32),
        ],
        compiler_params=pltpu.CompilerParams(
            dimension_semantics=("arbitrary",), vmem_limit_bytes=VMEM_LIMIT_BYTES),
        name="out_proj",
    )(y, x2, w_out_bf, g2, w_query_bf, sub_keys_bf)


def _top_desc(s, k):
    out = []
    for _ in range(k):
        mx = jnp.max(s, axis=0, keepdims=True)
        out.append(mx)
        s = jnp.where(s == mx, -jnp.inf, s)
    return out


def _batcher_network(n):
    pairs, p = [], 1
    while p < n:
        k = p
        while k >= 1:
            for j in range(k % p, n - k, 2 * k):
                for i in range(min(k, n - j - k)):
                    if (i + j) // (2 * p) == (i + j + k) // (2 * p):
                        pairs.append((i + j, i + j + k))
            k //= 2
        p *= 2
    return pairs


ROUTE_SUBLANES = 8
ROUTE_LISTS = N_KEYS // ROUTE_SUBLANES
ROUTE_NET = _batcher_network(ROUTE_LISTS)
ROUTE_RANKS = PEER_TOPK + 1


def _top_desc_keys(s, k):
    lists = [s[i * ROUTE_SUBLANES:(i + 1) * ROUTE_SUBLANES] for i in range(ROUTE_LISTS)]
    for a, b in ROUTE_NET:
        lists[a], lists[b] = jnp.maximum(lists[a], lists[b]), jnp.minimum(lists[a], lists[b])
    lists.append(jnp.full_like(lists[0], -jnp.inf))
    out = []
    for t in range(k):
        mx = jnp.max(lists[0], axis=0, keepdims=True)
        out.append(mx)
        if t + 1 < k:
            hit = lists[0] == mx
            depth = min(k - 1 - t, ROUTE_LISTS)
            for d in range(depth):
                lists[d] = jnp.where(hit, lists[d + 1], lists[d])
    return out


def _route_kernel(s1_ref, s2_ref, thr_ref, p1_ref, p2_ref):
    pairs = [(i, j) for i in range(ROUTE_RANKS) for j in range(ROUTE_RANKS)
             if (i + 1) * (j + 1) <= ROUTE_RANKS]
    for h in range(PEER_HEADS):
        s1 = s1_ref[h]
        s2 = s2_ref[h]
        v1 = _top_desc_keys(s1, ROUTE_RANKS)
        v2 = _top_desc_keys(s2, ROUTE_RANKS)
        cand = jnp.concatenate([v1[i] + v2[j] for i, j in pairs], axis=0)
        top = _top_desc(cand, ROUTE_RANKS)
        tau = 0.5 * (top[PEER_TOPK - 1] + top[PEER_TOPK])
        m1, m2 = v1[0], v2[0]
        lhs = jnp.concatenate([v2[j] for _, j in pairs], axis=0)
        rhs = jnp.concatenate([tau - v1[i] for i, _ in pairs], axis=0)
        z = jnp.sum(jnp.where(lhs >= rhs, jnp.exp(cand - (m1 + m2)), 0.0),
                    axis=0, keepdims=True)
        thr = tau - s1
        p1 = jnp.exp(s1 - m1) / z
        for c in range(thr.shape[1] // 128):
            thr_ref[c, h] = thr[:, c * 128:(c + 1) * 128]
            p1_ref[c, h] = p1[:, c * 128:(c + 1) * 128]
        p2_ref[h] = jnp.exp(s2 - m2)


def _route(s1t, s2t, tr):
    T = s1t.shape[-1]
    st = pl.BlockSpec((PEER_HEADS, N_KEYS, tr), lambda i: (0, 0, i))
    full = jax.ShapeDtypeStruct((PEER_HEADS, N_KEYS, T), F32)
    chunked = jax.ShapeDtypeStruct((T // 128, PEER_HEADS, N_KEYS, 128), F32)
    ct = pl.BlockSpec((tr // 128, PEER_HEADS, N_KEYS, 128), lambda i: (i, 0, 0, 0))
    return pl.pallas_call(
        _route_kernel,
        grid=(T // tr,),
        in_specs=[st, st],
        out_specs=[ct, ct, st],
        out_shape=[chunked, chunked, full],
        compiler_params=pltpu.CompilerParams(
            dimension_semantics=("arbitrary",), vmem_limit_bytes=VMEM_LIMIT_BYTES),
        name="route",
    )(s1t, s2t)


PEER_SUBLANES = 8
PEER_MXU_COLS = 256
PEER_KEY_ROWS = 64


def _peer_kernel(h2_ref, dn_ref, upt_ref, thr_ref, p1_ref, s2_ref, p2_ref, o_ref,
                 a_scr, c_scr, *, rows_e1, n_etiles, n_tiles):
    s = pl.program_id(0)
    tm = o_ref.shape[1]

    @pl.when(s == 0)
    def _():
        a_scr[...] = jnp.zeros_like(a_scr)
        c_scr[...] = jnp.zeros_like(c_scr)

    @pl.when((s == 0) | ((s - 2) % n_etiles == 0))
    def _():
        o_ref[...] = jnp.zeros_like(o_ref)

    t2 = jnp.clip(s - 1, 0, n_tiles - 1)
    row0 = ((t2 % n_etiles) * rows_e1) % PEER_SUBLANES

    for qc in range(tm // PEER_MXU_COLS):
        qsl = slice(qc * PEER_MXU_COLS, (qc + 1) * PEER_MXU_COLS)
        o_ref[:, qsl] += jnp.dot(upt_ref[...], c_scr[:, qsl], preferred_element_type=F32)
        for lc in range(PEER_MXU_COLS // 128):
            lcg = qc * (PEER_MXU_COLS // 128) + lc
            lsl = slice(lcg * 128, (lcg + 1) * 128)
            for k0 in range(0, N_KEYS, PEER_KEY_ROWS):
                ksl = slice(k0, k0 + PEER_KEY_ROWS)
                w = [jnp.zeros((PEER_KEY_ROWS, 128), F32) for _ in range(rows_e1)]
                for h in range(PEER_HEADS):
                    s2 = s2_ref[h, ksl, lsl]
                    p2 = p2_ref[h, ksl, lsl]
                    for r in range(rows_e1):
                        bcast = pl.ds(row0 + r, PEER_KEY_ROWS, stride=0)
                        w[r] = w[r] + jnp.where(s2 >= thr_ref[lcg, h, bcast, :],
                                                p1_ref[lcg, h, bcast, :] * p2, 0.0)
                for r in range(rows_e1):
                    esl = slice(r * N_KEYS + k0, r * N_KEYS + k0 + PEER_KEY_ROWS)
                    c_scr[esl, lsl] = (w[r] * _gelu(a_scr[esl, lsl])).astype(BF16)
        a_scr[:, qsl] = lax.dot_general(
            dn_ref[...], h2_ref[qsl, :], (((1,), (1,)), ((), ())), preferred_element_type=F32)


def _peer(h2, dn_bf, upt_bf, thr, p1t, s2t, p2t, tm, rows_e1):
    T = h2.shape[0]
    E = rows_e1 * N_KEYS
    n_etiles = N_EXPERTS // E
    n_tiles = (T // tm) * n_etiles
    assert tm % PEER_MXU_COLS == 0 and PEER_SUBLANES % rows_e1 == 0

    def tile(delay):
        def f(s):
            t = jnp.clip(s - delay, 0, n_tiles - 1)
            return t // n_etiles, t % n_etiles
        return f

    st1, st2, st3 = tile(0), tile(1), tile(2)
    rows_blk = pl.BlockSpec(
        (tm // 128, PEER_HEADS, PEER_SUBLANES, 128),
        lambda s: (st2(s)[0], 0, (st2(s)[1] * rows_e1) // PEER_SUBLANES, 0))
    keys_blk = pl.BlockSpec((PEER_HEADS, N_KEYS, tm), lambda s: (0, 0, st2(s)[0]),
                            pipeline_mode=pl.Buffered(1))
    return pl.pallas_call(
        functools.partial(_peer_kernel, rows_e1=rows_e1, n_etiles=n_etiles, n_tiles=n_tiles),
        grid=(n_tiles + 2,),
        in_specs=[
            pl.BlockSpec((tm, D_MODEL), lambda s: (st1(s)[0], 0)),
            pl.BlockSpec((E, D_MODEL), lambda s: (st1(s)[1], 0)),
            pl.BlockSpec((D_MODEL, E), lambda s: (0, st3(s)[1])),
            rows_blk, rows_blk, keys_blk, keys_blk,
        ],
        out_specs=pl.BlockSpec((D_MODEL, tm), lambda s: (0, st3(s)[0])),
        out_shape=jax.ShapeDtypeStruct((D_MODEL, T), F32),
        scratch_shapes=[pltpu.VMEM((E, tm), F32), pltpu.VMEM((E, tm), BF16)],
        compiler_params=pltpu.CompilerParams(
            dimension_semantics=("arbitrary",), vmem_limit_bytes=VMEM_LIMIT_BYTES),
        name="peer",
    )(h2, dn_bf, upt_bf, thr, p1t, s2t, p2t)


def _final_kernel(x1_ref, pt_ref, g_ref, o_ref):
    xo = x1_ref[...] + pt_ref[...].T
    o_ref[...] = _rmsnorm(xo, g_ref[...])


def _final(x1, peer_t, gf, tm):
    T = x1.shape[0]
    return pl.pallas_call(
        _final_kernel,
        grid=(T // tm,),
        in_specs=[pl.BlockSpec((tm, D_MODEL), lambda i: (i, 0)),
                  pl.BlockSpec((D_MODEL, tm), lambda i: (0, i)),
                  _resident((1, D_MODEL))],
        out_specs=pl.BlockSpec((tm, D_MODEL), lambda i: (i, 0)),
        out_shape=jax.ShapeDtypeStruct((T, D_MODEL), F32),
        compiler_params=pltpu.CompilerParams(
            dimension_semantics=("arbitrary",), vmem_limit_bytes=VMEM_LIMIT_BYTES),
        name="final",
    )(x1, peer_t, gf)


def kernel(x, norm1_g, w_in, sgu_ln_g, sgu_ln_b, w_spatial, b_spatial, attn_sinks, w_out,
           norm2_g, w_query, sub_keys, expert_down, expert_up, norm_f_g):
    B, S, D = x.shape
    assert D == D_MODEL and S % WINDOW == 0 and norm1_g.shape[0] == 1
    T = B * S
    nb = S // WINDOW
    x2 = x.reshape(T, D)

    u, v, q, kv = _in_proj(x2, norm1_g[0].reshape(1, D), w_in[0].astype(BF16), tm=512)

    bs_full = jnp.repeat(b_spatial[0].T, SGU_DIM, axis=1)
    y = _mix(attn_sinks[0], u, v, q, kv,
             sgu_ln_g[0].reshape(1, SGU_WIDTH), sgu_ln_b[0].reshape(1, SGU_WIDTH),
             w_spatial[0], bs_full, B, nb)

    x1, h2, s1t, s2t = _out_proj(y, x2, w_out[0].astype(BF16), norm2_g[0].reshape(1, D),
                                 w_query[0].astype(BF16), sub_keys[0].astype(BF16), tm=256)

    thr, p1t, p2t = _route(s1t, s2t, tr=256)

    peer_t = _peer(h2, expert_down[0].astype(BF16), expert_up[0].T.astype(BF16),
                   thr, p1t, s2t, p2t, tm=1024, rows_e1=4)

    out = _final(x1, peer_t, norm_f_g.reshape(1, D), tm=256)
    return out.reshape(B, S, D)
```

```python
import functools
import math

import jax
import jax.numpy as jnp
from jax import lax
from jax.experimental import pallas as pl
from jax.experimental.pallas import tpu as pltpu

D_MODEL = 2048
CHUNK = 128
SGU_GROUPS = 8
SGU_DIM = 128
SGU_WIDTH = SGU_GROUPS * SGU_DIM
HEAD_DIM = 64
N_Q_HEADS = 16
N_KV_HEADS = 2
Q_PER_KV = N_Q_HEADS // N_KV_HEADS
WINDOW = 128
ATTN_WIDTH = N_Q_HEADS * HEAD_DIM
KV_WIDTH = N_KV_HEADS * HEAD_DIM
IN_WIDTH = 2 * SGU_WIDTH + ATTN_WIDTH + 2 * KV_WIDTH
PEER_HEADS = 8
N_KEYS = 128
N_EXPERTS = N_KEYS * N_KEYS
PEER_TOPK = 16
D_QUERY = 256
EPS = 1e-6

VMEM_LIMIT_BYTES = 56 * 1024 * 1024

BF16 = jnp.bfloat16
F32 = jnp.float32


def _gelu(x):
    return 0.5 * x * (1.0 + lax.erf(x * np_sqrt_half))


np_sqrt_half = math.sqrt(0.5)


def _rmsnorm(x, g):
    return x * lax.rsqrt(jnp.mean(x * x, axis=-1, keepdims=True) + EPS) * g


def _resident(shape):
    nd = len(shape)
    return pl.BlockSpec(shape, lambda *_: (0,) * nd, pipeline_mode=pl.Buffered(1))


def _in_proj_kernel(x_ref, g_ref, w_ref, u_ref, v_ref, q_ref, kv_ref):
    h = _rmsnorm(x_ref[...], g_ref[...]).astype(BF16)
    o0, o1, o2, o3 = SGU_WIDTH, 2 * SGU_WIDTH, 2 * SGU_WIDTH + ATTN_WIDTH, IN_WIDTH
    zu = jnp.dot(h, w_ref[:, 0:o0], preferred_element_type=F32)
    u_ref[...] = _gelu(zu).astype(BF16)
    zv = jnp.dot(h, w_ref[:, o0:o1], preferred_element_type=F32)
    v_ref[...] = _gelu(zv).astype(BF16)
    q_ref[...] = jnp.dot(h, w_ref[:, o1:o2], preferred_element_type=F32).astype(BF16)
    kv_ref[...] = jnp.dot(h, w_ref[:, o2:o3], preferred_element_type=F32).astype(BF16)


def _in_proj(x2, g1, w_in_bf, tm):
    T = x2.shape[0]
    row = lambda w: pl.BlockSpec((tm, w), lambda i: (i, 0))
    return pl.pallas_call(
        _in_proj_kernel,
        grid=(T // tm,),
        in_specs=[row(D_MODEL), _resident((1, D_MODEL)), _resident((D_MODEL, IN_WIDTH))],
        out_specs=[row(SGU_WIDTH), row(SGU_WIDTH), row(ATTN_WIDTH), row(2 * KV_WIDTH)],
        out_shape=[
            jax.ShapeDtypeStruct((T, SGU_WIDTH), BF16),
            jax.ShapeDtypeStruct((T, SGU_WIDTH), BF16),
            jax.ShapeDtypeStruct((T, ATTN_WIDTH), BF16),
            jax.ShapeDtypeStruct((T, 2 * KV_WIDTH), BF16),
        ],
        compiler_params=pltpu.CompilerParams(
            dimension_semantics=("arbitrary",), vmem_limit_bytes=VMEM_LIMIT_BYTES),
        name="in_proj",
    )(x2, g1, w_in_bf)


def _mix_kernel(sink_ref, u_ref, v_ref, q_ref, kv_ref, kvp_ref, lng_ref, lnb_ref,
                ws_ref, bs_ref, y_ref):
    n = pl.program_id(1)
    row_t = lax.broadcasted_iota(jnp.int32, (CHUNK, CHUNK), 0)
    col_s = lax.broadcasted_iota(jnp.int32, (CHUNK, CHUNK), 1)
    causal = col_s <= row_t
    for g in range(SGU_GROUPS):
        sl = slice(g * SGU_DIM, (g + 1) * SGU_DIM)
        vg = v_ref[:, sl].astype(F32)
        mu = jnp.mean(vg, axis=-1, keepdims=True)
        var = jnp.mean(jnp.square(vg - mu), axis=-1, keepdims=True)
        vn = ((vg - mu) * lax.rsqrt(var + EPS) * lng_ref[:, sl] + lnb_ref[:, sl]).astype(BF16)
        w = jnp.where(causal, ws_ref[g], 0.0).astype(BF16)
        s = jnp.dot(w, vn, preferred_element_type=F32) + bs_ref[:, sl]
        y_ref[:, sl] = (u_ref[:, sl].astype(F32) * s).astype(BF16)

    qi = lax.broadcasted_iota(jnp.int32, (WINDOW, 2 * WINDOW), 0)
    kj = lax.broadcasted_iota(jnp.int32, (WINDOW, 2 * WINDOW), 1)
    band = (kj >= qi + 1) & (kj <= qi + WINDOW)
    mask = band & ((n > 0) | (kj >= WINDOW))
    scale = HEAD_DIM ** -0.5
    for j in range(N_KV_HEADS):
        ksl = slice(j * HEAD_DIM, (j + 1) * HEAD_DIM)
        vsl = slice(KV_WIDTH + j * HEAD_DIM, KV_WIDTH + (j + 1) * HEAD_DIM)
        kk = jnp.concatenate([kvp_ref[:, ksl], kv_ref[:, ksl]], axis=0)
        vv = jnp.concatenate([kvp_ref[:, vsl], kv_ref[:, vsl]], axis=0)
        for gq in range(Q_PER_KV):
            hq = j * Q_PER_KV + gq
            qh = q_ref[:, hq * HEAD_DIM:(hq + 1) * HEAD_DIM]
            sc = lax.dot_general(qh, kk, (((1,), (1,)), ((), ())),
                                 preferred_element_type=F32) * scale
            sc = jnp.where(mask, sc, -jnp.inf)
            sink = sink_ref[hq]
            m = jnp.maximum(jnp.max(sc, axis=-1, keepdims=True), sink)
            p = jnp.exp(sc - m)
            denom = jnp.sum(p, axis=-1, keepdims=True) + jnp.exp(sink - m)
            probs = (p / denom).astype(BF16)
            oh = jnp.dot(probs, vv, preferred_element_type=F32)
            c0 = SGU_WIDTH + hq * HEAD_DIM
            y_ref[:, c0:c0 + HEAD_DIM] = oh.astype(BF16)


def _mix(sinks, u, v, q, kv, lng, lnb, ws, bs_full, batch, nb):
    T = u.shape[0]
    cur = lambda w: pl.BlockSpec((WINDOW, w), lambda b, n: (b * nb + n, 0))
    prev = pl.BlockSpec((WINDOW, 2 * KV_WIDTH),
                        lambda b, n: (b * nb + jnp.maximum(n - 1, 0), 0))
    return pl.pallas_call(
        _mix_kernel,
        grid=(batch, nb),
        in_specs=[
            pl.BlockSpec(memory_space=pltpu.SMEM),
            cur(SGU_WIDTH), cur(SGU_WIDTH), cur(ATTN_WIDTH), cur(2 * KV_WIDTH), prev,
            _resident((1, SGU_WIDTH)), _resident((1, SGU_WIDTH)),
            _resident((SGU_GROUPS, CHUNK, CHUNK)), _resident((CHUNK, SGU_WIDTH)),
        ],
        out_specs=cur(D_MODEL),
        out_shape=jax.ShapeDtypeStruct((T, D_MODEL), BF16),
        compiler_params=pltpu.CompilerParams(
            dimension_semantics=("arbitrary", "arbitrary"), vmem_limit_bytes=VMEM_LIMIT_BYTES),
        name="mix",
    )(sinks, u, v, q, kv, kv, lng, lnb, ws, bs_full)


def _out_proj_kernel(y_ref, x_ref, wo_ref, g2_ref, wq_ref, sk_ref,
                     x1_ref, h2t_ref, s1_ref, s2_ref):
    x1 = x_ref[...] + jnp.dot(y_ref[...], wo_ref[...], preferred_element_type=F32)
    x1_ref[...] = x1
    h2 = _rmsnorm(x1, g2_ref[...]).astype(BF16)
    h2t_ref[...] = h2.T
    qry = jnp.dot(h2, wq_ref[...], preferred_element_type=F32).astype(BF16)
    half = D_QUERY // 2
    for h in range(PEER_HEADS):
        for c, s_ref in ((0, s1_ref), (1, s2_ref)):
            qc = qry[:, h * D_QUERY + c * half: h * D_QUERY + (c + 1) * half]
            s_ref[h] = lax.dot_general(sk_ref[c], qc, (((1,), (1,)), ((), ())),
                                       preferred_element_type=F32)


def _out_proj(y, x2, w_out_bf, g2, w_query_bf, sub_keys_bf, tm):
    T = x2.shape[0]
    row = lambda w: pl.BlockSpec((tm, w), lambda i: (i, 0))
    st = pl.BlockSpec((PEER_HEADS, N_KEYS, tm), lambda i: (0, 0, i))
    return pl.pallas_call(
        _out_proj_kernel,
        grid=(T // tm,),
        in_specs=[row(D_MODEL), row(D_MODEL), _resident((D_MODEL, D_MODEL)),
                  _resident((1, D_MODEL)), _resident((D_MODEL, PEER_HEADS * D_QUERY)),
                  _resident((2, N_KEYS, D_QUERY // 2))],
        out_specs=[row(D_MODEL), pl.BlockSpec((D_MODEL, tm), lambda i: (0, i)), st, st],
        out_shape=[
            jax.ShapeDtypeStruct((T, D_MODEL), F32),
            jax.ShapeDtypeStruct((D_MODEL, T), BF16),
            jax.ShapeDtypeStruct((PEER_HEADS, N_KEYS, T), F32),
            jax.ShapeDtypeStruct((PEER_HEADS, N_KEYS, T), F32),
        ],
        compiler_params=pltpu.CompilerParams(
            dimension_semantics=("arbitrary",), vmem_limit_bytes=VMEM_LIMIT_BYTES),
        name="out_proj",
    )(y, x2, w_out_bf, g2, w_query_bf, sub_keys_bf)


def _top_desc(s, k):
    out = []
    for _ in range(k):
        mx = jnp.max(s, axis=0, keepdims=True)
        out.append(mx)
        s = jnp.where(s == mx, -jnp.inf, s)
    return out


def _batcher_network(n):
    pairs, p = [], 1
    while p < n:
        k = p
        while k >= 1:
            for j in range(k % p, n - k, 2 * k):
                for i in range(min(k, n - j - k)):
                    if (i + j) // (2 * p) == (i + j + k) // (2 * p):
                        pairs.append((i + j, i + j + k))
            k //= 2
        p *= 2
    return pairs


ROUTE_SUBLANES = 8
ROUTE_LISTS = N_KEYS // ROUTE_SUBLANES
ROUTE_NET = _batcher_network(ROUTE_LISTS)
ROUTE_RANKS = PEER_TOPK + 1


def _top_desc_keys(s, k):
    lists = [s[i * ROUTE_SUBLANES:(i + 1) * ROUTE_SUBLANES] for i in range(ROUTE_LISTS)]
    for a, b in ROUTE_NET:
        lists[a], lists[b] = jnp.maximum(lists[a], lists[b]), jnp.minimum(lists[a], lists[b])
    lists.append(jnp.full_like(lists[0], -jnp.inf))
    out = []
    for t in range(k):
        mx = jnp.max(lists[0], axis=0, keepdims=True)
        out.append(mx)
        if t + 1 < k:
            hit = lists[0] == mx
            depth = min(k - 1 - t, ROUTE_LISTS)
            for d in range(depth):
                lists[d] = jnp.where(hit, lists[d + 1], lists[d])
    return out


def _route_kernel(s1_ref, s2_ref, thr_ref, p1_ref, p2_ref):
    pairs = [(i, j) for i in range(ROUTE_RANKS) for j in range(ROUTE_RANKS)
             if (i + 1) * (j + 1) <= ROUTE_RANKS]
    for h in range(PEER_HEADS):
        s1 = s1_ref[h]
        s2 = s2_ref[h]
        v1 = _top_desc_keys(s1, ROUTE_RANKS)
        v2 = _top_desc_keys(s2, ROUTE_RANKS)
        cand = jnp.concatenate([v1[i] + v2[j] for i, j in pairs], axis=0)
        top = _top_desc(cand, ROUTE_RANKS)
        tau = 0.5 * (top[PEER_TOPK - 1] + top[PEER_TOPK])
        m1, m2 = v1[0], v2[0]
        lhs = jnp.concatenate([jnp.exp(v2[j] - m2) for _, j in pairs], axis=0)
        rhs = jnp.concatenate([jnp.exp((tau - v1[i]) - m2) for i, _ in pairs], axis=0)
        z = jnp.sum(jnp.where(lhs >= rhs, jnp.exp(cand - (m1 + m2)), 0.0),
                    axis=0, keepdims=True)
        pthr = jnp.exp((tau - s1) - m2)
        p1 = 0.5 * (jnp.exp(s1 - m1) / z)
        for c in range(pthr.shape[1] // 128):
            thr_ref[c, h] = pthr[:, c * 128:(c + 1) * 128]
            p1_ref[c, h] = p1[:, c * 128:(c + 1) * 128]
        p2_ref[h] = jnp.exp(s2 - m2)


def _route(s1t, s2t, tr):
    T = s1t.shape[-1]
    st = pl.BlockSpec((PEER_HEADS, N_KEYS, tr), lambda i: (0, 0, i))
    full = jax.ShapeDtypeStruct((PEER_HEADS, N_KEYS, T), F32)
    chunked = jax.ShapeDtypeStruct((T // 128, PEER_HEADS, N_KEYS, 128), F32)
    ct = pl.BlockSpec((tr // 128, PEER_HEADS, N_KEYS, 128), lambda i: (i, 0, 0, 0))
    return pl.pallas_call(
        _route_kernel,
        grid=(T // tr,),
        in_specs=[st, st],
        out_specs=[ct, ct, st],
        out_shape=[chunked, chunked, full],
        compiler_params=pltpu.CompilerParams(
            dimension_semantics=("arbitrary",), vmem_limit_bytes=VMEM_LIMIT_BYTES),
        name="route",
    )(s1t, s2t)


PEER_SUBLANES = 8
PEER_MXU_COLS = 256
PEER_KEY_ROWS = 16


def _peer_kernel(h2t_ref, dn_ref, upt_ref, thr_ref, p1_ref, p2_ref, o_ref,
                 a_scr, c_scr, *, rows_e1, n_etiles, n_tiles):
    s = pl.program_id(0)
    tm = o_ref.shape[1]

    @pl.when(s == 0)
    def _():
        a_scr[...] = jnp.zeros_like(a_scr)
        c_scr[...] = jnp.zeros_like(c_scr)

    @pl.when((s == 0) | ((s - 2) % n_etiles == 0))
    def _():
        o_ref[...] = jnp.zeros_like(o_ref)

    t2 = jnp.clip(s - 1, 0, n_tiles - 1)
    row0 = ((t2 % n_etiles) * rows_e1) % PEER_SUBLANES

    for qc in range(tm // PEER_MXU_COLS):
        qsl = slice(qc * PEER_MXU_COLS, (qc + 1) * PEER_MXU_COLS)
        o_ref[:, qsl] += jnp.dot(upt_ref[...], c_scr[:, qsl], preferred_element_type=F32)
        for lc in range(PEER_MXU_COLS // 128):
            lcg = qc * (PEER_MXU_COLS // 128) + lc
            lsl = slice(lcg * 128, (lcg + 1) * 128)
            for k0 in range(0, N_KEYS, PEER_KEY_ROWS):
                ksl = slice(k0, k0 + PEER_KEY_ROWS)
                w = [jnp.zeros((PEER_KEY_ROWS, 128), F32) for _ in range(rows_e1)]
                for h in range(PEER_HEADS):
                    p2 = p2_ref[h, ksl, lsl]
                    for r in range(rows_e1):
                        bcast = pl.ds(row0 + r, PEER_KEY_ROWS, stride=0)
                        w[r] = w[r] + jnp.where(p2 >= thr_ref[lcg, h, bcast, :],
                                                p1_ref[lcg, h, bcast, :] * p2, 0.0)
                for r in range(rows_e1):
                    esl = slice(r * N_KEYS + k0, r * N_KEYS + k0 + PEER_KEY_ROWS)
                    a = a_scr[esl, lsl]
                    c_scr[esl, lsl] = (w[r] * (a * (1.0 + lax.erf(a * np_sqrt_half)))).astype(BF16)
        a_scr[:, qsl] = jnp.dot(dn_ref[...], h2t_ref[:, qsl], preferred_element_type=F32)


def _peer(h2t, dn_bf, upt_bf, thr, p1t, p2t, tm, rows_e1):
    T = h2t.shape[1]
    E = rows_e1 * N_KEYS
    n_etiles = N_EXPERTS // E
    n_tiles = (T // tm) * n_etiles
    assert tm % PEER_MXU_COLS == 0 and PEER_SUBLANES % rows_e1 == 0

    def tile(delay):
        def f(s):
            t = jnp.clip(s - delay, 0, n_tiles - 1)
            return t // n_etiles, t % n_etiles
        return f

    st1, st2, st3 = tile(0), tile(1), tile(2)
    rows_blk = pl.BlockSpec(
        (tm // 128, PEER_HEADS, PEER_SUBLANES, 128),
        lambda s: (st2(s)[0], 0, (st2(s)[1] * rows_e1) // PEER_SUBLANES, 0))
    once = pl.Buffered(1)
    return pl.pallas_call(
        functools.partial(_peer_kernel, rows_e1=rows_e1, n_etiles=n_etiles, n_tiles=n_tiles),
        grid=(n_tiles + 2,),
        in_specs=[
            pl.BlockSpec((D_MODEL, tm), lambda s: (0, st1(s)[0]), pipeline_mode=once),
            pl.BlockSpec((E, D_MODEL), lambda s: (st1(s)[1], 0)),
            pl.BlockSpec((D_MODEL, E), lambda s: (0, st3(s)[1])),
            rows_blk, rows_blk,
            pl.BlockSpec((PEER_HEADS, N_KEYS, tm), lambda s: (0, 0, st2(s)[0]),
                         pipeline_mode=once),
        ],
        out_specs=pl.BlockSpec((D_MODEL, tm), lambda s: (0, st3(s)[0]), pipeline_mode=once),
        out_shape=jax.ShapeDtypeStruct((D_MODEL, T), F32),
        scratch_shapes=[pltpu.VMEM((E, tm), F32), pltpu.VMEM((E, tm), BF16)],
        compiler_params=pltpu.CompilerParams(
            dimension_semantics=("arbitrary",), vmem_limit_bytes=VMEM_LIMIT_BYTES),
        name="peer",
    )(h2t, dn_bf, upt_bf, thr, p1t, p2t)


def _final_kernel(x1_ref, pt_ref, g_ref, o_ref):
    xo = x1_ref[...] + pt_ref[...].T
    o_ref[...] = _rmsnorm(xo, g_ref[...])


def _final(x1, peer_t, gf, tm):
    T = x1.shape[0]
    return pl.pallas_call(
        _final_kernel,
        grid=(T // tm,),
        in_specs=[pl.BlockSpec((tm, D_MODEL), lambda i: (i, 0)),
                  pl.BlockSpec((D_MODEL, tm), lambda i: (0, i)),
                  _resident((1, D_MODEL))],
        out_specs=pl.BlockSpec((tm, D_MODEL), lambda i: (i, 0)),
        out_shape=jax.ShapeDtypeStruct((T, D_MODEL), F32),
        compiler_params=pltpu.CompilerParams(
            dimension_semantics=("arbitrary",), vmem_limit_bytes=VMEM_LIMIT_BYTES),
        name="final",
    )(x1, peer_t, gf)


def kernel(x, norm1_g, w_in, sgu_ln_g, sgu_ln_b, w_spatial, b_spatial, attn_sinks, w_out,
           norm2_g, w_query, sub_keys, expert_down, expert_up, norm_f_g):
    B, S, D = x.shape
    assert D == D_MODEL and S % WINDOW == 0 and norm1_g.shape[0] == 1
    T = B * S
    nb = S // WINDOW
    x2 = x.reshape(T, D)

    u, v, q, kv = _in_proj(x2, norm1_g[0].reshape(1, D), w_in[0].astype(BF16), tm=512)

    bs_full = jnp.repeat(b_spatial[0].T, SGU_DIM, axis=1)
    y = _mix(attn_sinks[0], u, v, q, kv,
             sgu_ln_g[0].reshape(1, SGU_WIDTH), sgu_ln_b[0].reshape(1, SGU_WIDTH),
             w_spatial[0], bs_full, B, nb)

    x1, h2t, s1t, s2t = _out_proj(y, x2, w_out[0].astype(BF16), norm2_g[0].reshape(1, D),
                                  w_query[0].astype(BF16), sub_keys[0].astype(BF16), tm=256)

    pthr, p1t, p2t = _route(s1t, s2t, tr=256)

    peer_t = _peer(h2t, expert_down[0].astype(BF16), expert_up[0].T.astype(BF16),
                   pthr, p1t, p2t, tm=1024, rows_e1=8)

    out = _final(x1, peer_t, norm_f_g.reshape(1, D), tm=256)
    return out.reshape(B, S, D)
```

```python
import functools
import math

import jax
import jax.numpy as jnp
from jax import lax
from jax.experimental import pallas as pl
from jax.experimental.pallas import tpu as pltpu

D_MODEL = 2048
CHUNK = 128
SGU_GROUPS = 8
SGU_DIM = 128
SGU_WIDTH = SGU_GROUPS * SGU_DIM
HEAD_DIM = 64
N_Q_HEADS = 16
N_KV_HEADS = 2
Q_PER_KV = N_Q_HEADS // N_KV_HEADS
WINDOW = 128
ATTN_WIDTH = N_Q_HEADS * HEAD_DIM
KV_WIDTH = N_KV_HEADS * HEAD_DIM
IN_WIDTH = 2 * SGU_WIDTH + ATTN_WIDTH + 2 * KV_WIDTH
PEER_HEADS = 8
N_KEYS = 128
N_EXPERTS = N_KEYS * N_KEYS
PEER_TOPK = 16
D_QUERY = 256
EPS = 1e-6

VMEM_LIMIT_BYTES = 56 * 1024 * 1024

BF16 = jnp.bfloat16
F32 = jnp.float32


def _gelu(x):
    return 0.5 * x * (1.0 + lax.erf(x * np_sqrt_half))


np_sqrt_half = math.sqrt(0.5)


def _rmsnorm(x, g):
    return x * lax.rsqrt(jnp.mean(x * x, axis=-1, keepdims=True) + EPS) * g


def _resident(shape):
    nd = len(shape)
    return pl.BlockSpec(shape, lambda *_: (0,) * nd, pipeline_mode=pl.Buffered(1))


def _in_proj_kernel(x_ref, g_ref, w_ref, u_ref, v_ref, q_ref, kv_ref):
    h = _rmsnorm(x_ref[...], g_ref[...]).astype(BF16)
    o0, o1, o2, o3 = SGU_WIDTH, 2 * SGU_WIDTH, 2 * SGU_WIDTH + ATTN_WIDTH, IN_WIDTH
    zu = jnp.dot(h, w_ref[:, 0:o0], preferred_element_type=F32)
    u_ref[...] = _gelu(zu).astype(BF16)
    zv = jnp.dot(h, w_ref[:, o0:o1], preferred_element_type=F32)
    v_ref[...] = _gelu(zv).astype(BF16)
    q_ref[...] = jnp.dot(h, w_ref[:, o1:o2], preferred_element_type=F32).astype(BF16)
    kv_ref[...] = jnp.dot(h, w_ref[:, o2:o3], preferred_element_type=F32).astype(BF16)


def _in_proj(x2, g1, w_in_bf, tm):
    T = x2.shape[0]
    row = lambda w: pl.BlockSpec((tm, w), lambda i: (i, 0))
    return pl.pallas_call(
        _in_proj_kernel,
        grid=(T // tm,),
        in_specs=[row(D_MODEL), _resident((1, D_MODEL)), _resident((D_MODEL, IN_WIDTH))],
        out_specs=[row(SGU_WIDTH), row(SGU_WIDTH), row(ATTN_WIDTH), row(2 * KV_WIDTH)],
        out_shape=[
            jax.ShapeDtypeStruct((T, SGU_WIDTH), BF16),
            jax.ShapeDtypeStruct((T, SGU_WIDTH), BF16),
            jax.ShapeDtypeStruct((T, ATTN_WIDTH), BF16),
            jax.ShapeDtypeStruct((T, 2 * KV_WIDTH), BF16),
        ],
        compiler_params=pltpu.CompilerParams(
            dimension_semantics=("arbitrary",), vmem_limit_bytes=VMEM_LIMIT_BYTES),
        name="in_proj",
    )(x2, g1, w_in_bf)


def _mix_kernel(sink_ref, u_ref, v_ref, q_ref, kv_ref, kvp_ref, lng_ref, lnb_ref,
                ws_ref, bs_ref, y_ref, *, nsub):
    n = pl.program_id(1)
    row_t = lax.broadcasted_iota(jnp.int32, (CHUNK, CHUNK), 0)
    col_s = lax.broadcasted_iota(jnp.int32, (CHUNK, CHUNK), 1)
    causal = col_s <= row_t
    qi = lax.broadcasted_iota(jnp.int32, (WINDOW, 2 * WINDOW), 0)
    kj = lax.broadcasted_iota(jnp.int32, (WINDOW, 2 * WINDOW), 1)
    band = (kj >= qi + 1) & (kj <= qi + WINDOW)
    scale = HEAD_DIM ** -0.5
    for sub in range(nsub):
        rows = slice(sub * WINDOW, (sub + 1) * WINDOW)
        for g in range(SGU_GROUPS):
            sl = slice(g * SGU_DIM, (g + 1) * SGU_DIM)
            vg = v_ref[rows, sl].astype(F32)
            mu = jnp.mean(vg, axis=-1, keepdims=True)
            var = jnp.mean(jnp.square(vg - mu), axis=-1, keepdims=True)
            vn = ((vg - mu) * lax.rsqrt(var + EPS) * lng_ref[:, sl] + lnb_ref[:, sl]).astype(BF16)
            w = jnp.where(causal, ws_ref[g], 0.0).astype(BF16)
            s = jnp.dot(w, vn, preferred_element_type=F32) + bs_ref[:, sl]
            y_ref[rows, sl] = (u_ref[rows, sl].astype(F32) * s).astype(BF16)

        if sub == 0:
            prev_ref, prows = kvp_ref, slice(0, WINDOW)
            mask = band & ((n > 0) | (kj >= WINDOW))
        else:
            prev_ref, prows = kv_ref, slice((sub - 1) * WINDOW, sub * WINDOW)
            mask = band
        for j in range(N_KV_HEADS):
            ksl = slice(j * HEAD_DIM, (j + 1) * HEAD_DIM)
            vsl = slice(KV_WIDTH + j * HEAD_DIM, KV_WIDTH + (j + 1) * HEAD_DIM)
            kk = jnp.concatenate([prev_ref[prows, ksl], kv_ref[rows, ksl]], axis=0)
            vv = jnp.concatenate([prev_ref[prows, vsl], kv_ref[rows, vsl]], axis=0)
            for gq in range(Q_PER_KV):
                hq = j * Q_PER_KV + gq
                qh = q_ref[rows, hq * HEAD_DIM:(hq + 1) * HEAD_DIM]
                sc = lax.dot_general(qh, kk, (((1,), (1,)), ((), ())),
                                     preferred_element_type=F32) * scale
                sc = jnp.where(mask, sc, -jnp.inf)
                sink = sink_ref[hq]
                m = jnp.maximum(jnp.max(sc, axis=-1, keepdims=True), sink)
                p = jnp.exp(sc - m)
                denom = jnp.sum(p, axis=-1, keepdims=True) + jnp.exp(sink - m)
                probs = (p / denom).astype(BF16)
                oh = jnp.dot(probs, vv, preferred_element_type=F32)
                c0 = SGU_WIDTH + hq * HEAD_DIM
                y_ref[rows, c0:c0 + HEAD_DIM] = oh.astype(BF16)


def _mix(sinks, u, v, q, kv, lng, lnb, ws, bs_full, batch, nb, nsub):
    T = u.shape[0]
    assert nb % nsub == 0
    steps = nb // nsub
    cur = lambda w: pl.BlockSpec((nsub * WINDOW, w), lambda b, n: (b * steps + n, 0))
    prev = pl.BlockSpec((WINDOW, 2 * KV_WIDTH),
                        lambda b, n: (jnp.maximum((b * steps + n) * nsub - 1, 0), 0))
    return pl.pallas_call(
        functools.partial(_mix_kernel, nsub=nsub),
        grid=(batch, steps),
        in_specs=[
            pl.BlockSpec(memory_space=pltpu.SMEM),
            cur(SGU_WIDTH), cur(SGU_WIDTH), cur(ATTN_WIDTH), cur(2 * KV_WIDTH), prev,
            _resident((1, SGU_WIDTH)), _resident((1, SGU_WIDTH)),
            _resident((SGU_GROUPS, CHUNK, CHUNK)), _resident((CHUNK, SGU_WIDTH)),
        ],
        out_specs=cur(D_MODEL),
        out_shape=jax.ShapeDtypeStruct((T, D_MODEL), BF16),
        compiler_params=pltpu.CompilerParams(
            dimension_semantics=("arbitrary", "arbitrary"), vmem_limit_bytes=VMEM_LIMIT_BYTES),
        name="mix",
    )(sinks, u, v, q, kv, kv, lng, lnb, ws, bs_full)


def _out_proj_kernel(y_ref, x_ref, wo_ref, g2_ref, wq_ref, sk_ref,
                     x1_ref, h2t_ref, s1_ref, s2_ref):
    x1 = x_ref[...] + jnp.dot(y_ref[...], wo_ref[...], preferred_element_type=F32)
    x1_ref[...] = x1
    h2 = _rmsnorm(x1, g2_ref[...]).astype(BF16)
    h2t_ref[...] = h2.T
    qry = jnp.dot(h2, wq_ref[...], preferred_element_type=F32).astype(BF16)
    half = D_QUERY // 2
    for h in range(PEER_HEADS):
        for c, s_ref in ((0, s1_ref), (1, s2_ref)):
            qc = qry[:, h * D_QUERY + c * half: h * D_QUERY + (c + 1) * half]
            s_ref[h] = lax.dot_general(sk_ref[c], qc, (((1,), (1,)), ((), ())),
                                       preferred_element_type=F32)


def _out_proj(y, x2, w_out_bf, g2, w_query_bf, sub_keys_bf, tm):
    T = x2.shape[0]
    row = lambda w: pl.BlockSpec((tm, w), lambda i: (i, 0))
    st = pl.BlockSpec((PEER_HEADS, N_KEYS, tm), lambda i: (0, 0, i))
    return pl.pallas_call(
        _out_proj_kernel,
        grid=(T // tm,),
        in_specs=[row(D_MODEL), row(D_MODEL), _resident((D_MODEL, D_MODEL)),
                  _resident((1, D_MODEL)), _resident((D_MODEL, PEER_HEADS * D_QUERY)),
                  _resident((2, N_KEYS, D_QUERY // 2))],
        out_specs=[row(D_MODEL), pl.BlockSpec((D_MODEL, tm), lambda i: (0, i)), st, st],
        out_shape=[
            jax.ShapeDtypeStruct((T, D_MODEL), F32),
            jax.ShapeDtypeStruct((D_MODEL, T), BF16),
            jax.ShapeDtypeStruct((PEER_HEADS, N_KEYS, T), F32),
            jax.ShapeDtypeStruct((PEER_HEADS, N_KEYS, T), F32),
        ],
        compiler_params=pltpu.CompilerParams(
            dimension_semantics=("arbitrary",), vmem_limit_bytes=VMEM_LIMIT_BYTES),
        name="out_proj",
    )(y, x2, w_out_bf, g2, w_query_bf, sub_keys_bf)


def _top_desc(s, k):
    out = []
    for _ in range(k):
        mx = jnp.max(s, axis=0, keepdims=True)
        out.append(mx)
        s = jnp.where(s == mx, -jnp.inf, s)
    return out


def _batcher_network(n):
    pairs, p = [], 1
    while p < n:
        k = p
        while k >= 1:
            for j in range(k % p, n - k, 2 * k):
                for i in range(min(k, n - j - k)):
                    if (i + j) // (2 * p) == (i + j + k) // (2 * p):
                        pairs.append((i + j, i + j + k))
            k //= 2
        p *= 2
    return pairs


ROUTE_SUBLANES = 8
ROUTE_LISTS = N_KEYS // ROUTE_SUBLANES
ROUTE_NET = _batcher_network(ROUTE_LISTS)
ROUTE_RANKS = PEER_TOPK + 1


def _top_desc_keys(s, k):
    lists = [s[i * ROUTE_SUBLANES:(i + 1) * ROUTE_SUBLANES] for i in range(ROUTE_LISTS)]
    for a, b in ROUTE_NET:
        lists[a], lists[b] = jnp.maximum(lists[a], lists[b]), jnp.minimum(lists[a], lists[b])
    lists.append(jnp.full_like(lists[0], -jnp.inf))
    out = []
    for t in range(k):
        mx = jnp.max(lists[0], axis=0, keepdims=True)
        out.append(mx)
        if t + 1 < k:
            hit = lists[0] == mx
            depth = min(k - 1 - t, ROUTE_LISTS)
            for d in range(depth):
                lists[d] = jnp.where(hit, lists[d + 1], lists[d])
    return out


def _route_kernel(s1_ref, s2_ref, thr_ref, p1_ref, p2_ref):
    pairs = [(i, j) for i in range(ROUTE_RANKS) for j in range(ROUTE_RANKS)
             if (i + 1) * (j + 1) <= ROUTE_RANKS]
    for h in range(PEER_HEADS):
        s1 = s1_ref[h]
        s2 = s2_ref[h]
        v1 = _top_desc_keys(s1, ROUTE_RANKS)
        v2 = _top_desc_keys(s2, ROUTE_RANKS)
        cand = jnp.concatenate([v1[i] + v2[j] for i, j in pairs], axis=0)
        top = _top_desc(cand, ROUTE_RANKS)
        tau = 0.5 * (top[PEER_TOPK - 1] + top[PEER_TOPK])
        m1, m2 = v1[0], v2[0]
        lhs = jnp.concatenate([jnp.exp(v2[j] - m2) for _, j in pairs], axis=0)
        rhs = jnp.concatenate([jnp.exp((tau - v1[i]) - m2) for i, _ in pairs], axis=0)
        z = jnp.sum(jnp.where(lhs >= rhs, jnp.exp(cand - (m1 + m2)), 0.0),
                    axis=0, keepdims=True)
        pthr = jnp.exp((tau - s1) - m2)
        p1 = 0.5 * (jnp.exp(s1 - m1) / z)
        for c in range(pthr.shape[1] // 128):
            thr_ref[c, h] = pthr[:, c * 128:(c + 1) * 128]
            p1_ref[c, h] = p1[:, c * 128:(c + 1) * 128]
        p2_ref[h] = jnp.exp(s2 - m2)


def _route(s1t, s2t, tr):
    T = s1t.shape[-1]
    st = pl.BlockSpec((PEER_HEADS, N_KEYS, tr), lambda i: (0, 0, i))
    full = jax.ShapeDtypeStruct((PEER_HEADS, N_KEYS, T), F32)
    chunked = jax.ShapeDtypeStruct((T // 128, PEER_HEADS, N_KEYS, 128), F32)
    ct = pl.BlockSpec((tr // 128, PEER_HEADS, N_KEYS, 128), lambda i: (i, 0, 0, 0))
    return pl.pallas_call(
        _route_kernel,
        grid=(T // tr,),
        in_specs=[st, st],
        out_specs=[ct, ct, st],
        out_shape=[chunked, chunked, full],
        compiler_params=pltpu.CompilerParams(
            dimension_semantics=("arbitrary",), vmem_limit_bytes=VMEM_LIMIT_BYTES),
        name="route",
    )(s1t, s2t)


PEER_SUBLANES = 8
PEER_MXU_COLS = 256
PEER_KEY_ROWS = 16
PEER_VMEM_LIMIT_BYTES = 58 * 1024 * 1024


def _peer_order(n_chunks):
    return [(stage, qc) for qc in range(n_chunks) for stage in (3, 2, 1)]


def _peer_kernel(h2t_ref, dn_ref, upt_ref, thr_ref, p1_ref, p2_ref, o_ref,
                 a_scr, c_scr, *, rows_e1, n_etiles, n_tiles):
    s = pl.program_id(0)
    tm = o_ref.shape[1]

    @pl.when(s == 0)
    def _():
        a_scr[...] = jnp.zeros_like(a_scr)
        c_scr[...] = jnp.zeros_like(c_scr)

    @pl.when((s == 0) | ((s - 2) % n_etiles == 0))
    def _():
        o_ref[...] = jnp.zeros_like(o_ref)

    t2 = jnp.clip(s - 1, 0, n_tiles - 1)
    row0 = ((t2 % n_etiles) * rows_e1) % PEER_SUBLANES

    def stage3(qc):
        qsl = slice(qc * PEER_MXU_COLS, (qc + 1) * PEER_MXU_COLS)
        o_ref[:, qsl] += jnp.dot(upt_ref[...], c_scr[:, qsl], preferred_element_type=F32)

    def stage2(qc):
        for lc in range(PEER_MXU_COLS // 128):
            lcg = qc * (PEER_MXU_COLS // 128) + lc
            lsl = slice(lcg * 128, (lcg + 1) * 128)
            for k0 in range(0, N_KEYS, PEER_KEY_ROWS):
                ksl = slice(k0, k0 + PEER_KEY_ROWS)
                w = [jnp.zeros((PEER_KEY_ROWS, 128), F32) for _ in range(rows_e1)]
                for h in range(PEER_HEADS):
                    p2 = p2_ref[h, ksl, lsl]
                    for r in range(rows_e1):
                        bcast = pl.ds(row0 + r, PEER_KEY_ROWS, stride=0)
                        w[r] = w[r] + jnp.where(p2 >= thr_ref[lcg, h, bcast, :],
                                                p1_ref[lcg, h, bcast, :] * p2, 0.0)
                for r in range(rows_e1):
                    esl = slice(r * N_KEYS + k0, r * N_KEYS + k0 + PEER_KEY_ROWS)
                    a = a_scr[esl, lsl]
                    c_scr[esl, lsl] = (w[r] * (a * (1.0 + lax.erf(a * np_sqrt_half)))).astype(BF16)

    def stage1(qc):
        qsl = slice(qc * PEER_MXU_COLS, (qc + 1) * PEER_MXU_COLS)
        a_scr[:, qsl] = jnp.dot(dn_ref[...], h2t_ref[:, qsl], preferred_element_type=F32)

    stages = {1: stage1, 2: stage2, 3: stage3}
    for stage, qc in _peer_order(tm // PEER_MXU_COLS):
        stages[stage](qc)


def _peer(h2t, dn_bf, upt_bf, thr, p1t, p2t, tm, rows_e1):
    T = h2t.shape[1]
    E = rows_e1 * N_KEYS
    n_etiles = N_EXPERTS // E
    n_tiles = (T // tm) * n_etiles
    assert tm % PEER_MXU_COLS == 0 and PEER_SUBLANES % rows_e1 == 0

    def tile(delay):
        def f(s):
            t = jnp.clip(s - delay, 0, n_tiles - 1)
            return t // n_etiles, t % n_etiles
        return f

    st1, st2, st3 = tile(0), tile(1), tile(2)
    rows_blk = pl.BlockSpec(
        (tm // 128, PEER_HEADS, PEER_SUBLANES, 128),
        lambda s: (st2(s)[0], 0, (st2(s)[1] * rows_e1) // PEER_SUBLANES, 0))
    once = pl.Buffered(1)
    return pl.pallas_call(
        functools.partial(_peer_kernel, rows_e1=rows_e1, n_etiles=n_etiles, n_tiles=n_tiles),
        grid=(n_tiles + 2,),
        in_specs=[
            pl.BlockSpec((D_MODEL, tm), lambda s: (0, st1(s)[0])),
            pl.BlockSpec((E, D_MODEL), lambda s: (st1(s)[1], 0)),
            pl.BlockSpec((None, D_MODEL, E), lambda s: (st3(s)[1], 0, 0)),
            rows_blk, rows_blk,
            pl.BlockSpec((PEER_HEADS, N_KEYS, tm), lambda s: (0, 0, st2(s)[0]),
                         pipeline_mode=once),
        ],
        out_specs=pl.BlockSpec((D_MODEL, tm), lambda s: (0, st3(s)[0])),
        out_shape=jax.ShapeDtypeStruct((D_MODEL, T), F32),
        scratch_shapes=[pltpu.VMEM((E, tm), F32), pltpu.VMEM((E, tm), BF16)],
        compiler_params=pltpu.CompilerParams(
            dimension_semantics=("arbitrary",), vmem_limit_bytes=PEER_VMEM_LIMIT_BYTES),
        name="peer",
    )(h2t, dn_bf, upt_bf, thr, p1t, p2t)


def _final_kernel(x1_ref, pt_ref, g_ref, o_ref):
    xo = x1_ref[...] + pt_ref[...].T
    o_ref[...] = _rmsnorm(xo, g_ref[...])


def _final(x1, peer_t, gf, tm):
    T = x1.shape[0]
    return pl.pallas_call(
        _final_kernel,
        grid=(T // tm,),
        in_specs=[pl.BlockSpec((tm, D_MODEL), lambda i: (i, 0)),
                  pl.BlockSpec((D_MODEL, tm), lambda i: (0, i)),
                  _resident((1, D_MODEL))],
        out_specs=pl.BlockSpec((tm, D_MODEL), lambda i: (i, 0)),
        out_shape=jax.ShapeDtypeStruct((T, D_MODEL), F32),
        compiler_params=pltpu.CompilerParams(
            dimension_semantics=("arbitrary",), vmem_limit_bytes=VMEM_LIMIT_BYTES),
        name="final",
    )(x1, peer_t, gf)


def kernel(x, norm1_g, w_in, sgu_ln_g, sgu_ln_b, w_spatial, b_spatial, attn_sinks, w_out,
           norm2_g, w_query, sub_keys, expert_down, expert_up, norm_f_g):
    B, S, D = x.shape
    assert D == D_MODEL and S % WINDOW == 0 and norm1_g.shape[0] == 1
    T = B * S
    nb = S // WINDOW
    x2 = x.reshape(T, D)

    u, v, q, kv = _in_proj(x2, norm1_g[0].reshape(1, D), w_in[0].astype(BF16), tm=512)

    bs_full = jnp.repeat(b_spatial[0].T, SGU_DIM, axis=1)
    y = _mix(attn_sinks[0], u, v, q, kv,
             sgu_ln_g[0].reshape(1, SGU_WIDTH), sgu_ln_b[0].reshape(1, SGU_WIDTH),
             w_spatial[0], bs_full, B, nb, nsub=1)

    x1, h2t, s1t, s2t = _out_proj(y, x2, w_out[0].astype(BF16), norm2_g[0].reshape(1, D),
                                  w_query[0].astype(BF16), sub_keys[0].astype(BF16), tm=256)

    pthr, p1t, p2t = _route(s1t, s2t, tr=256)

    rows_e1 = 8
    upt = expert_up[0].reshape(N_KEYS // rows_e1, rows_e1 * N_KEYS, D).transpose(0, 2, 1)
    peer_t = _peer(h2t, expert_down[0].astype(BF16), upt.astype(BF16),
                   pthr, p1t, p2t, tm=1024, rows_e1=rows_e1)

    out = _final(x1, peer_t, norm_f_g.reshape(1, D), tm=256)
    return out.reshape(B, S, D)
```

```python
import functools
import math

import jax
import jax.numpy as jnp
from jax import lax
from jax.experimental import pallas as pl
from jax.experimental.pallas import tpu as pltpu

D_MODEL = 2048
CHUNK = 128
SGU_GROUPS = 8
SGU_DIM = 128
SGU_WIDTH = SGU_GROUPS * SGU_DIM
HEAD_DIM = 64
N_Q_HEADS = 16
N_KV_HEADS = 2
Q_PER_KV = N_Q_HEADS // N_KV_HEADS
WINDOW = 128
ATTN_WIDTH = N_Q_HEADS * HEAD_DIM
KV_WIDTH = N_KV_HEADS * HEAD_DIM
IN_WIDTH = 2 * SGU_WIDTH + ATTN_WIDTH + 2 * KV_WIDTH
PEER_HEADS = 8
N_KEYS = 128
N_EXPERTS = N_KEYS * N_KEYS
PEER_TOPK = 16
D_QUERY = 256
EPS = 1e-6

VMEM_LIMIT_BYTES = 56 * 1024 * 1024

BF16 = jnp.bfloat16
F32 = jnp.float32


def _gelu(x):
    return 0.5 * x * (1.0 + lax.erf(x * np_sqrt_half))


np_sqrt_half = math.sqrt(0.5)


def _rmsnorm(x, g):
    return x * lax.rsqrt(jnp.mean(x * x, axis=-1, keepdims=True) + EPS) * g


def _resident(shape):
    nd = len(shape)
    return pl.BlockSpec(shape, lambda *_: (0,) * nd, pipeline_mode=pl.Buffered(1))


def _in_proj_kernel(x_ref, g_ref, w_ref, dn_ref, u_ref, v_ref, q_ref, kv_ref, dnb_ref):
    dnb_ref[...] = dn_ref[...].astype(BF16)
    h = _rmsnorm(x_ref[...], g_ref[...]).astype(BF16)
    o0, o1, o2, o3 = SGU_WIDTH, 2 * SGU_WIDTH, 2 * SGU_WIDTH + ATTN_WIDTH, IN_WIDTH
    zu = jnp.dot(h, w_ref[:, 0:o0], preferred_element_type=F32)
    u_ref[...] = _gelu(zu).astype(BF16)
    zv = jnp.dot(h, w_ref[:, o0:o1], preferred_element_type=F32)
    v_ref[...] = _gelu(zv).astype(BF16)
    q_ref[...] = jnp.dot(h, w_ref[:, o1:o2], preferred_element_type=F32).astype(BF16)
    kv_ref[...] = jnp.dot(h, w_ref[:, o2:o3], preferred_element_type=F32).astype(BF16)


def _expert_slab(steps):
    assert N_EXPERTS % steps == 0 and (N_EXPERTS // steps) % 128 == 0
    return N_EXPERTS // steps


def _in_proj(x2, g1, w_in_bf, expert_down, tm):
    T = x2.shape[0]
    steps = T // tm
    row = lambda w: pl.BlockSpec((tm, w), lambda i: (i, 0))
    slab_spec = pl.BlockSpec((_expert_slab(steps), D_MODEL), lambda i: (i, 0))
    return pl.pallas_call(
        _in_proj_kernel,
        grid=(steps,),
        in_specs=[row(D_MODEL), _resident((1, D_MODEL)), _resident((D_MODEL, IN_WIDTH)),
                  slab_spec],
        out_specs=[row(SGU_WIDTH), row(SGU_WIDTH), row(ATTN_WIDTH), row(2 * KV_WIDTH),
                   slab_spec],
        out_shape=[
            jax.ShapeDtypeStruct((T, SGU_WIDTH), BF16),
            jax.ShapeDtypeStruct((T, SGU_WIDTH), BF16),
            jax.ShapeDtypeStruct((T, ATTN_WIDTH), BF16),
            jax.ShapeDtypeStruct((T, 2 * KV_WIDTH), BF16),
            jax.ShapeDtypeStruct((N_EXPERTS, D_MODEL), BF16),
        ],
        compiler_params=pltpu.CompilerParams(
            dimension_semantics=("arbitrary",), vmem_limit_bytes=VMEM_LIMIT_BYTES),
        name="in_proj",
    )(x2, g1, w_in_bf, expert_down)


def _mix_kernel(sink_ref, u_ref, v_ref, q_ref, kv_ref, kvp_ref, lng_ref, lnb_ref,
                ws_ref, bs_ref, y_ref, *, nsub):
    n = pl.program_id(1)
    row_t = lax.broadcasted_iota(jnp.int32, (CHUNK, CHUNK), 0)
    col_s = lax.broadcasted_iota(jnp.int32, (CHUNK, CHUNK), 1)
    causal = col_s <= row_t
    qi = lax.broadcasted_iota(jnp.int32, (WINDOW, 2 * WINDOW), 0)
    kj = lax.broadcasted_iota(jnp.int32, (WINDOW, 2 * WINDOW), 1)
    band = (kj >= qi + 1) & (kj <= qi + WINDOW)
    scale = HEAD_DIM ** -0.5
    for sub in range(nsub):
        rows = slice(sub * WINDOW, (sub + 1) * WINDOW)
        for g in range(SGU_GROUPS):
            sl = slice(g * SGU_DIM, (g + 1) * SGU_DIM)
            vg = v_ref[rows, sl].astype(F32)
            mu = jnp.mean(vg, axis=-1, keepdims=True)
            var = jnp.mean(jnp.square(vg - mu), axis=-1, keepdims=True)
            vn = ((vg - mu) * lax.rsqrt(var + EPS) * lng_ref[:, sl] + lnb_ref[:, sl]).astype(BF16)
            w = jnp.where(causal, ws_ref[g], 0.0).astype(BF16)
            s = jnp.dot(w, vn, preferred_element_type=F32) + bs_ref[:, sl]
            y_ref[rows, sl] = (u_ref[rows, sl].astype(F32) * s).astype(BF16)

        if sub == 0:
            prev_ref, prows = kvp_ref, slice(0, WINDOW)
            mask = band & ((n > 0) | (kj >= WINDOW))
        else:
            prev_ref, prows = kv_ref, slice((sub - 1) * WINDOW, sub * WINDOW)
            mask = band
        for j in range(N_KV_HEADS):
            ksl = slice(j * HEAD_DIM, (j + 1) * HEAD_DIM)
            vsl = slice(KV_WIDTH + j * HEAD_DIM, KV_WIDTH + (j + 1) * HEAD_DIM)
            kk = jnp.concatenate([prev_ref[prows, ksl], kv_ref[rows, ksl]], axis=0)
            vv = jnp.concatenate([prev_ref[prows, vsl], kv_ref[rows, vsl]], axis=0)
            for gq in range(Q_PER_KV):
                hq = j * Q_PER_KV + gq
                qh = q_ref[rows, hq * HEAD_DIM:(hq + 1) * HEAD_DIM]
                sc = lax.dot_general(qh, kk, (((1,), (1,)), ((), ())),
                                     preferred_element_type=F32) * scale
                sc = jnp.where(mask, sc, -jnp.inf)
                sink = sink_ref[hq]
                m = jnp.maximum(jnp.max(sc, axis=-1, keepdims=True), sink)
                p = jnp.exp(sc - m)
                denom = jnp.sum(p, axis=-1, keepdims=True) + jnp.exp(sink - m)
                probs = (p / denom).astype(BF16)
                oh = jnp.dot(probs, vv, preferred_element_type=F32)
                c0 = SGU_WIDTH + hq * HEAD_DIM
                y_ref[rows, c0:c0 + HEAD_DIM] = oh.astype(BF16)


def _mix(sinks, u, v, q, kv, lng, lnb, ws, bs_full, batch, nb, nsub):
    T = u.shape[0]
    assert nb % nsub == 0
    steps = nb // nsub
    cur = lambda w: pl.BlockSpec((nsub * WINDOW, w), lambda b, n: (b * steps + n, 0))
    prev = pl.BlockSpec((WINDOW, 2 * KV_WIDTH),
                        lambda b, n: (jnp.maximum((b * steps + n) * nsub - 1, 0), 0))
    return pl.pallas_call(
        functools.partial(_mix_kernel, nsub=nsub),
        grid=(batch, steps),
        in_specs=[
            pl.BlockSpec(memory_space=pltpu.SMEM),
            cur(SGU_WIDTH), cur(SGU_WIDTH), cur(ATTN_WIDTH), cur(2 * KV_WIDTH), prev,
            _resident((1, SGU_WIDTH)), _resident((1, SGU_WIDTH)),
            _resident((SGU_GROUPS, CHUNK, CHUNK)), _resident((CHUNK, SGU_WIDTH)),
        ],
        out_specs=cur(D_MODEL),
        out_shape=jax.ShapeDtypeStruct((T, D_MODEL), BF16),
        compiler_params=pltpu.CompilerParams(
            dimension_semantics=("arbitrary", "arbitrary"), vmem_limit_bytes=VMEM_LIMIT_BYTES),
        name="mix",
    )(sinks, u, v, q, kv, kv, lng, lnb, ws, bs_full)


def _out_proj_kernel(y_ref, x_ref, wo_ref, g2_ref, wq_ref, sk_ref, up_ref,
                     x1_ref, h2t_ref, s1_ref, s2_ref, upt_ref):
    upt_ref[...] = up_ref[...].astype(BF16).T
    x1 = x_ref[...] + jnp.dot(y_ref[...], wo_ref[...], preferred_element_type=F32)
    x1_ref[...] = x1
    h2 = _rmsnorm(x1, g2_ref[...]).astype(BF16)
    h2t_ref[...] = h2.T
    qry = jnp.dot(h2, wq_ref[...], preferred_element_type=F32).astype(BF16)
    half = D_QUERY // 2
    for h in range(PEER_HEADS):
        for c, s_ref in ((0, s1_ref), (1, s2_ref)):
            qc = qry[:, h * D_QUERY + c * half: h * D_QUERY + (c + 1) * half]
            s_ref[h] = lax.dot_general(sk_ref[c], qc, (((1,), (1,)), ((), ())),
                                       preferred_element_type=F32)


def _out_proj(y, x2, w_out_bf, g2, w_query_bf, sub_keys_bf, expert_up, tm, peer_tile):
    T = x2.shape[0]
    steps = T // tm
    slab = _expert_slab(steps)
    assert peer_tile % slab == 0
    per_tile = peer_tile // slab
    row = lambda w: pl.BlockSpec((tm, w), lambda i: (i, 0))
    st = pl.BlockSpec((PEER_HEADS, N_KEYS, tm), lambda i: (0, 0, i))
    return pl.pallas_call(
        _out_proj_kernel,
        grid=(steps,),
        in_specs=[row(D_MODEL), row(D_MODEL), _resident((D_MODEL, D_MODEL)),
                  _resident((1, D_MODEL)), _resident((D_MODEL, PEER_HEADS * D_QUERY)),
                  _resident((2, N_KEYS, D_QUERY // 2)),
                  pl.BlockSpec((slab, D_MODEL), lambda i: (i, 0))],
        out_specs=[row(D_MODEL), pl.BlockSpec((D_MODEL, tm), lambda i: (0, i)), st, st,
                   pl.BlockSpec((None, D_MODEL, slab), lambda i: (i // per_tile, 0, i % per_tile))],
        out_shape=[
            jax.ShapeDtypeStruct((T, D_MODEL), F32),
            jax.ShapeDtypeStruct((D_MODEL, T), BF16),
            jax.ShapeDtypeStruct((PEER_HEADS, N_KEYS, T), F32),
            jax.ShapeDtypeStruct((PEER_HEADS, N_KEYS, T), F32),
            jax.ShapeDtypeStruct((N_EXPERTS // peer_tile, D_MODEL, peer_tile), BF16),
        ],
        compiler_params=pltpu.CompilerParams(
            dimension_semantics=("arbitrary",), vmem_limit_bytes=VMEM_LIMIT_BYTES),
        name="out_proj",
    )(y, x2, w_out_bf, g2, w_query_bf, sub_keys_bf, expert_up)


def _top_desc(s, k):
    out = []
    for _ in range(k):
        mx = jnp.max(s, axis=0, keepdims=True)
        out.append(mx)
        s = jnp.where(s == mx, -jnp.inf, s)
    return out


def _batcher_network(n):
    pairs, p = [], 1
    while p < n:
        k = p
        while k >= 1:
            for j in range(k % p, n - k, 2 * k):
                for i in range(min(k, n - j - k)):
                    if (i + j) // (2 * p) == (i + j + k) // (2 * p):
                        pairs.append((i + j, i + j + k))
            k //= 2
        p *= 2
    return pairs


ROUTE_SUBLANES = 8
ROUTE_LISTS = N_KEYS // ROUTE_SUBLANES
ROUTE_NET = _batcher_network(ROUTE_LISTS)
ROUTE_RANKS = PEER_TOPK + 1


def _top_desc_keys(s, k):
    lists = [s[i * ROUTE_SUBLANES:(i + 1) * ROUTE_SUBLANES] for i in range(ROUTE_LISTS)]
    for a, b in ROUTE_NET:
        lists[a], lists[b] = jnp.maximum(lists[a], lists[b]), jnp.minimum(lists[a], lists[b])
    lists.append(jnp.full_like(lists[0], -jnp.inf))
    out = []
    for t in range(k):
        mx = jnp.max(lists[0], axis=0, keepdims=True)
        out.append(mx)
        if t + 1 < k:
            hit = lists[0] == mx
            depth = min(k - 1 - t, ROUTE_LISTS)
            for d in range(depth):
                lists[d] = jnp.where(hit, lists[d + 1], lists[d])
    return out


def _route_kernel(s1_ref, s2_ref, thr_ref, p1_ref, p2_ref):
    pairs = [(i, j) for i in range(ROUTE_RANKS) for j in range(ROUTE_RANKS)
             if (i + 1) * (j + 1) <= ROUTE_RANKS]
    for h in range(PEER_HEADS):
        s1 = s1_ref[h]
        s2 = s2_ref[h]
        v1 = _top_desc_keys(s1, ROUTE_RANKS)
        v2 = _top_desc_keys(s2, ROUTE_RANKS)
        cand = jnp.concatenate([v1[i] + v2[j] for i, j in pairs], axis=0)
        top = _top_desc(cand, ROUTE_RANKS)
        tau = 0.5 * (top[PEER_TOPK - 1] + top[PEER_TOPK])
        m1, m2 = v1[0], v2[0]
        lhs = jnp.concatenate([jnp.exp(v2[j] - m2) for _, j in pairs], axis=0)
        rhs = jnp.concatenate([jnp.exp((tau - v1[i]) - m2) for i, _ in pairs], axis=0)
        z = jnp.sum(jnp.where(lhs >= rhs, jnp.exp(cand - (m1 + m2)), 0.0),
                    axis=0, keepdims=True)
        pthr = jnp.exp((tau - s1) - m2)
        p1 = 0.5 * (jnp.exp(s1 - m1) / z)
        for c in range(pthr.shape[1] // 128):
            thr_ref[c, h] = pthr[:, c * 128:(c + 1) * 128]
            p1_ref[c, h] = p1[:, c * 128:(c + 1) * 128]
        p2_ref[h] = jnp.exp(s2 - m2)


def _route(s1t, s2t, tr):
    T = s1t.shape[-1]
    st = pl.BlockSpec((PEER_HEADS, N_KEYS, tr), lambda i: (0, 0, i))
    full = jax.ShapeDtypeStruct((PEER_HEADS, N_KEYS, T), F32)
    chunked = jax.ShapeDtypeStruct((T // 128, PEER_HEADS, N_KEYS, 128), F32)
    ct = pl.BlockSpec((tr // 128, PEER_HEADS, N_KEYS, 128), lambda i: (i, 0, 0, 0))
    return pl.pallas_call(
        _route_kernel,
        grid=(T // tr,),
        in_specs=[st, st],
        out_specs=[ct, ct, st],
        out_shape=[chunked, chunked, full],
        compiler_params=pltpu.CompilerParams(
            dimension_semantics=("arbitrary",), vmem_limit_bytes=VMEM_LIMIT_BYTES),
        name="route",
    )(s1t, s2t)


PEER_SUBLANES = 8
PEER_MXU_COLS = 256
PEER_KEY_ROWS = 16
PEER_VMEM_LIMIT_BYTES = 58 * 1024 * 1024


def _peer_order(n_chunks):
    return [(stage, qc) for qc in range(n_chunks) for stage in (3, 2, 1)]


def _peer_kernel(h2t_ref, dn_ref, upt_ref, thr_ref, p1_ref, p2_ref, o_ref,
                 a_scr, c_scr, *, rows_e1, n_etiles, n_tiles):
    s = pl.program_id(0)
    tm = o_ref.shape[1]

    @pl.when(s == 0)
    def _():
        a_scr[...] = jnp.zeros_like(a_scr)
        c_scr[...] = jnp.zeros_like(c_scr)

    @pl.when((s == 0) | ((s - 2) % n_etiles == 0))
    def _():
        o_ref[...] = jnp.zeros_like(o_ref)

    t2 = jnp.clip(s - 1, 0, n_tiles - 1)
    row0 = ((t2 % n_etiles) * rows_e1) % PEER_SUBLANES

    def stage3(qc):
        qsl = slice(qc * PEER_MXU_COLS, (qc + 1) * PEER_MXU_COLS)
        o_ref[:, qsl] += jnp.dot(upt_ref[...], c_scr[:, qsl], preferred_element_type=F32)

    def stage2(qc):
        for lc in range(PEER_MXU_COLS // 128):
            lcg = qc * (PEER_MXU_COLS // 128) + lc
            lsl = slice(lcg * 128, (lcg + 1) * 128)
            for k0 in range(0, N_KEYS, PEER_KEY_ROWS):
                ksl = slice(k0, k0 + PEER_KEY_ROWS)
                w = [jnp.zeros((PEER_KEY_ROWS, 128), F32) for _ in range(rows_e1)]
                for h in range(PEER_HEADS):
                    p2 = p2_ref[h, ksl, lsl]
                    for r in range(rows_e1):
                        bcast = pl.ds(row0 + r, PEER_KEY_ROWS, stride=0)
                        w[r] = w[r] + jnp.where(p2 >= thr_ref[lcg, h, bcast, :],
                                                p1_ref[lcg, h, bcast, :] * p2, 0.0)
                for r in range(rows_e1):
                    esl = slice(r * N_KEYS + k0, r * N_KEYS + k0 + PEER_KEY_ROWS)
                    a = a_scr[esl, lsl]
                    c_scr[esl, lsl] = (w[r] * (a * (1.0 + lax.erf(a * np_sqrt_half)))).astype(BF16)

    def stage1(qc):
        qsl = slice(qc * PEER_MXU_COLS, (qc + 1) * PEER_MXU_COLS)
        a_scr[:, qsl] = jnp.dot(dn_ref[...], h2t_ref[:, qsl], preferred_element_type=F32)

    stages = {1: stage1, 2: stage2, 3: stage3}
    for stage, qc in _peer_order(tm // PEER_MXU_COLS):
        stages[stage](qc)


def _peer(h2t, dn_bf, upt_bf, thr, p1t, p2t, tm, rows_e1):
    T = h2t.shape[1]
    E = rows_e1 * N_KEYS
    n_etiles = N_EXPERTS // E
    n_tiles = (T // tm) * n_etiles
    assert tm % PEER_MXU_COLS == 0 and PEER_SUBLANES % rows_e1 == 0

    def tile(delay):
        def f(s):
            t = jnp.clip(s - delay, 0, n_tiles - 1)
            return t // n_etiles, t % n_etiles
        return f

    st1, st2, st3 = tile(0), tile(1), tile(2)
    rows_blk = pl.BlockSpec(
        (tm // 128, PEER_HEADS, PEER_SUBLANES, 128),
        lambda s: (st2(s)[0], 0, (st2(s)[1] * rows_e1) // PEER_SUBLANES, 0))
    once = pl.Buffered(1)
    return pl.pallas_call(
        functools.partial(_peer_kernel, rows_e1=rows_e1, n_etiles=n_etiles, n_tiles=n_tiles),
        grid=(n_tiles + 2,),
        in_specs=[
            pl.BlockSpec((D_MODEL, tm), lambda s: (0, st1(s)[0])),
            pl.BlockSpec((E, D_MODEL), lambda s: (st1(s)[1], 0)),
            pl.BlockSpec((None, D_MODEL, E), lambda s: (st3(s)[1], 0, 0)),
            rows_blk, rows_blk,
            pl.BlockSpec((PEER_HEADS, N_KEYS, tm), lambda s: (0, 0, st2(s)[0]),
                         pipeline_mode=once),
        ],
        out_specs=pl.BlockSpec((D_MODEL, tm), lambda s: (0, st3(s)[0])),
        out_shape=jax.ShapeDtypeStruct((D_MODEL, T), F32),
        scratch_shapes=[pltpu.VMEM((E, tm), F32), pltpu.VMEM((E, tm), BF16)],
        compiler_params=pltpu.CompilerParams(
            dimension_semantics=("arbitrary",), vmem_limit_bytes=PEER_VMEM_LIMIT_BYTES),
        name="peer",
    )(h2t, dn_bf, upt_bf, thr, p1t, p2t)


def _final_kernel(x1_ref, pt_ref, g_ref, o_ref):
    xo = x1_ref[...] + pt_ref[...].T
    o_ref[...] = _rmsnorm(xo, g_ref[...])


def _final(x1, peer_t, gf, tm):
    T = x1.shape[0]
    return pl.pallas_call(
        _final_kernel,
        grid=(T // tm,),
        in_specs=[pl.BlockSpec((tm, D_MODEL), lambda i: (i, 0)),
                  pl.BlockSpec((D_MODEL, tm), lambda i: (0, i)),
                  _resident((1, D_MODEL))],
        out_specs=pl.BlockSpec((tm, D_MODEL), lambda i: (i, 0)),
        out_shape=jax.ShapeDtypeStruct((T, D_MODEL), F32),
        compiler_params=pltpu.CompilerParams(
            dimension_semantics=("arbitrary",), vmem_limit_bytes=VMEM_LIMIT_BYTES),
        name="final",
    )(x1, peer_t, gf)


def kernel(x, norm1_g, w_in, sgu_ln_g, sgu_ln_b, w_spatial, b_spatial, attn_sinks, w_out,
           norm2_g, w_query, sub_keys, expert_down, expert_up, norm_f_g):
    B, S, D = x.shape
    assert D == D_MODEL and S % WINDOW == 0 and norm1_g.shape[0] == 1
    T = B * S
    nb = S // WINDOW
    x2 = x.reshape(T, D)

    rows_e1 = 8
    u, v, q, kv, dn_bf = _in_proj(
        x2, norm1_g[0].reshape(1, D), w_in[0].astype(BF16), expert_down[0], tm=512)

    bs_full = jnp.repeat(b_spatial[0].T, SGU_DIM, axis=1)
    y = _mix(attn_sinks[0], u, v, q, kv,
             sgu_ln_g[0].reshape(1, SGU_WIDTH), sgu_ln_b[0].reshape(1, SGU_WIDTH),
             w_spatial[0], bs_full, B, nb, nsub=1)

    x1, h2t, s1t, s2t, upt_bf = _out_proj(
        y, x2, w_out[0].astype(BF16), norm2_g[0].reshape(1, D), w_query[0].astype(BF16),
        sub_keys[0].astype(BF16), expert_up[0], tm=256, peer_tile=rows_e1 * N_KEYS)

    pthr, p1t, p2t = _route(s1t, s2t, tr=256)

    peer_t = _peer(h2t, dn_bf, upt_bf, pthr, p1t, p2t, tm=1024, rows_e1=rows_e1)

    out = _final(x1, peer_t, norm_f_g.reshape(1, D), tm=256)
    return out.reshape(B, S, D)
```

```python
import functools
import math

import jax
import jax.numpy as jnp
from jax import lax
from jax.experimental import pallas as pl
from jax.experimental.pallas import tpu as pltpu

D_MODEL = 2048
CHUNK = 128
SGU_GROUPS = 8
SGU_DIM = 128
SGU_WIDTH = SGU_GROUPS * SGU_DIM
HEAD_DIM = 64
N_Q_HEADS = 16
N_KV_HEADS = 2
Q_PER_KV = N_Q_HEADS // N_KV_HEADS
WINDOW = 128
ATTN_WIDTH = N_Q_HEADS * HEAD_DIM
KV_WIDTH = N_KV_HEADS * HEAD_DIM
IN_WIDTH = 2 * SGU_WIDTH + ATTN_WIDTH + 2 * KV_WIDTH
PEER_HEADS = 8
N_KEYS = 128
N_EXPERTS = N_KEYS * N_KEYS
PEER_TOPK = 16
D_QUERY = 256
EPS = 1e-6
ATTN_SCALE = HEAD_DIM ** -0.5

VMEM_LIMIT_BYTES = 56 * 1024 * 1024

BF16 = jnp.bfloat16
F32 = jnp.float32


def _gelu(x):
    return 0.5 * x * (1.0 + lax.erf(x * np_sqrt_half))


np_sqrt_half = math.sqrt(0.5)


def _rmsnorm(x, g):
    return x * lax.rsqrt(jnp.mean(x * x, axis=-1, keepdims=True) + EPS) * g


def _resident(shape):
    nd = len(shape)
    return pl.BlockSpec(shape, lambda *_: (0,) * nd, pipeline_mode=pl.Buffered(1))


def _in_proj_kernel(x_ref, g_ref, w_ref, dn_ref, u_ref, v_ref, q_ref, kv_ref, dnb_ref):
    dnb_ref[...] = dn_ref[...].astype(BF16)
    h = _rmsnorm(x_ref[...], g_ref[...]).astype(BF16)
    o0, o1, o2, o3 = SGU_WIDTH, 2 * SGU_WIDTH, 2 * SGU_WIDTH + ATTN_WIDTH, IN_WIDTH
    zu = jnp.dot(h, w_ref[:, 0:o0], preferred_element_type=F32)
    u_ref[...] = _gelu(zu).astype(BF16)
    zv = jnp.dot(h, w_ref[:, o0:o1], preferred_element_type=F32)
    v_ref[...] = _gelu(zv).astype(BF16)
    q_ref[...] = jnp.dot(h, w_ref[:, o1:o2], preferred_element_type=F32).astype(BF16)
    kv_ref[...] = jnp.dot(h, w_ref[:, o2:o3], preferred_element_type=F32).astype(BF16)


def _expert_slab(steps):
    assert N_EXPERTS % steps == 0 and (N_EXPERTS // steps) % 128 == 0
    return N_EXPERTS // steps


def _in_proj(x2, g1, w_in_bf, expert_down, tm):
    T = x2.shape[0]
    steps = T // tm
    row = lambda w: pl.BlockSpec((tm, w), lambda i: (i, 0))
    slab_spec = pl.BlockSpec((_expert_slab(steps), D_MODEL), lambda i: (i, 0))
    return pl.pallas_call(
        _in_proj_kernel,
        grid=(steps,),
        in_specs=[row(D_MODEL), _resident((1, D_MODEL)), _resident((D_MODEL, IN_WIDTH)),
                  slab_spec],
        out_specs=[row(SGU_WIDTH), row(SGU_WIDTH), row(ATTN_WIDTH), row(2 * KV_WIDTH),
                   slab_spec],
        out_shape=[
            jax.ShapeDtypeStruct((T, SGU_WIDTH), BF16),
            jax.ShapeDtypeStruct((T, SGU_WIDTH), BF16),
            jax.ShapeDtypeStruct((T, ATTN_WIDTH), BF16),
            jax.ShapeDtypeStruct((T, 2 * KV_WIDTH), BF16),
            jax.ShapeDtypeStruct((N_EXPERTS, D_MODEL), BF16),
        ],
        compiler_params=pltpu.CompilerParams(
            dimension_semantics=("arbitrary",), vmem_limit_bytes=VMEM_LIMIT_BYTES),
        name="in_proj",
    )(x2, g1, w_in_bf, expert_down)


def _mix_kernel(sink_ref, u_ref, v_ref, q_ref, kv_ref, kvp_ref, lng_ref, lnb_ref,
                ws_ref, bs_ref, y_ref, *, nsub):
    n = pl.program_id(1)
    row_t = lax.broadcasted_iota(jnp.int32, (CHUNK, CHUNK), 0)
    col_s = lax.broadcasted_iota(jnp.int32, (CHUNK, CHUNK), 1)
    causal = col_s <= row_t
    qi = lax.broadcasted_iota(jnp.int32, (WINDOW, 2 * WINDOW), 0)
    kj = lax.broadcasted_iota(jnp.int32, (WINDOW, 2 * WINDOW), 1)
    band = (kj >= qi + 1) & (kj <= qi + WINDOW)
    for sub in range(nsub):
        rows = slice(sub * WINDOW, (sub + 1) * WINDOW)
        for g in range(SGU_GROUPS):
            sl = slice(g * SGU_DIM, (g + 1) * SGU_DIM)
            vg = v_ref[rows, sl].astype(F32)
            mu = jnp.mean(vg, axis=-1, keepdims=True)
            var = jnp.mean(jnp.square(vg - mu), axis=-1, keepdims=True)
            vn = ((vg - mu) * lax.rsqrt(var + EPS) * lng_ref[:, sl] + lnb_ref[:, sl]).astype(BF16)
            w = jnp.where(causal, ws_ref[g], 0.0).astype(BF16)
            s = jnp.dot(w, vn, preferred_element_type=F32) + bs_ref[:, sl]
            y_ref[rows, sl] = (u_ref[rows, sl].astype(F32) * s).astype(BF16)

        if sub == 0:
            prev_ref, prows = kvp_ref, slice(0, WINDOW)
            mask = band & ((n > 0) | (kj >= WINDOW))
        else:
            prev_ref, prows = kv_ref, slice((sub - 1) * WINDOW, sub * WINDOW)
            mask = band
        for j in range(N_KV_HEADS):
            ksl = slice(j * HEAD_DIM, (j + 1) * HEAD_DIM)
            vsl = slice(KV_WIDTH + j * HEAD_DIM, KV_WIDTH + (j + 1) * HEAD_DIM)
            kk = jnp.concatenate([prev_ref[prows, ksl], kv_ref[rows, ksl]], axis=0)
            vv = jnp.concatenate([prev_ref[prows, vsl], kv_ref[rows, vsl]], axis=0)
            for gq in range(Q_PER_KV):
                hq = j * Q_PER_KV + gq
                qh = q_ref[rows, hq * HEAD_DIM:(hq + 1) * HEAD_DIM]
                sc = lax.dot_general(qh, kk, (((1,), (1,)), ((), ())),
                                     preferred_element_type=F32) * ATTN_SCALE
                sc = jnp.where(mask, sc, -jnp.inf)
                sink = sink_ref[hq]
                m = jnp.maximum(jnp.max(sc, axis=-1, keepdims=True), sink)
                p = jnp.exp(sc - m)
                denom = jnp.sum(p, axis=-1, keepdims=True) + jnp.exp(sink - m)
                probs = (p / denom).astype(BF16)
                oh = jnp.dot(probs, vv, preferred_element_type=F32)
                c0 = SGU_WIDTH + hq * HEAD_DIM
                y_ref[rows, c0:c0 + HEAD_DIM] = oh.astype(BF16)


def _mix(sinks, u, v, q, kv, lng, lnb, ws, bs_full, batch, nb, nsub):
    T = u.shape[0]
    assert nb % nsub == 0
    steps = nb // nsub
    cur = lambda w: pl.BlockSpec((nsub * WINDOW, w), lambda b, n: (b * steps + n, 0))
    prev = pl.BlockSpec((WINDOW, 2 * KV_WIDTH),
                        lambda b, n: (jnp.maximum((b * steps + n) * nsub - 1, 0), 0))
    return pl.pallas_call(
        functools.partial(_mix_kernel, nsub=nsub),
        grid=(batch, steps),
        in_specs=[
            pl.BlockSpec(memory_space=pltpu.SMEM),
            cur(SGU_WIDTH), cur(SGU_WIDTH), cur(ATTN_WIDTH), cur(2 * KV_WIDTH), prev,
            _resident((1, SGU_WIDTH)), _resident((1, SGU_WIDTH)),
            _resident((SGU_GROUPS, CHUNK, CHUNK)), _resident((CHUNK, SGU_WIDTH)),
        ],
        out_specs=cur(D_MODEL),
        out_shape=jax.ShapeDtypeStruct((T, D_MODEL), BF16),
        compiler_params=pltpu.CompilerParams(
            dimension_semantics=("arbitrary", "arbitrary"), vmem_limit_bytes=VMEM_LIMIT_BYTES),
        name="mix",
    )(sinks, u, v, q, kv, kv, lng, lnb, ws, bs_full)


def _out_proj_kernel(y_ref, x_ref, wo_ref, g2_ref, wq_ref, sk_ref, up_ref,
                     x1_ref, h2t_ref, s1_ref, s2_ref, upt_ref):
    upt_ref[...] = up_ref[...].astype(BF16).T
    x1 = x_ref[...] + jnp.dot(y_ref[...], wo_ref[...], preferred_element_type=F32)
    x1_ref[...] = x1
    h2 = _rmsnorm(x1, g2_ref[...]).astype(BF16)
    h2t_ref[...] = h2.T
    qry = jnp.dot(h2, wq_ref[...], preferred_element_type=F32).astype(BF16)
    half = D_QUERY // 2
    for h in range(PEER_HEADS):
        for c, s_ref in ((0, s1_ref), (1, s2_ref)):
            qc = qry[:, h * D_QUERY + c * half: h * D_QUERY + (c + 1) * half]
            s_ref[h] = lax.dot_general(sk_ref[c], qc, (((1,), (1,)), ((), ())),
                                       preferred_element_type=F32)


def _out_proj(y, x2, w_out_bf, g2, w_query_bf, sub_keys_bf, expert_up, tm, peer_tile):
    T = x2.shape[0]
    steps = T // tm
    slab = _expert_slab(steps)
    assert peer_tile % slab == 0
    per_tile = peer_tile // slab
    row = lambda w: pl.BlockSpec((tm, w), lambda i: (i, 0))
    st = pl.BlockSpec((PEER_HEADS, N_KEYS, tm), lambda i: (0, 0, i))
    return pl.pallas_call(
        _out_proj_kernel,
        grid=(steps,),
        in_specs=[row(D_MODEL), row(D_MODEL), _resident((D_MODEL, D_MODEL)),
                  _resident((1, D_MODEL)), _resident((D_MODEL, PEER_HEADS * D_QUERY)),
                  _resident((2, N_KEYS, D_QUERY // 2)),
                  pl.BlockSpec((slab, D_MODEL), lambda i: (i, 0))],
        out_specs=[row(D_MODEL), pl.BlockSpec((D_MODEL, tm), lambda i: (0, i)), st, st,
                   pl.BlockSpec((None, D_MODEL, slab), lambda i: (i // per_tile, 0, i % per_tile))],
        out_shape=[
            jax.ShapeDtypeStruct((T, D_MODEL), F32),
            jax.ShapeDtypeStruct((D_MODEL, T), BF16),
            jax.ShapeDtypeStruct((PEER_HEADS, N_KEYS, T), F32),
            jax.ShapeDtypeStruct((PEER_HEADS, N_KEYS, T), F32),
            jax.ShapeDtypeStruct((N_EXPERTS // peer_tile, D_MODEL, peer_tile), BF16),
        ],
        compiler_params=pltpu.CompilerParams(
            dimension_semantics=("arbitrary",), vmem_limit_bytes=VMEM_LIMIT_BYTES),
        name="out_proj",
    )(y, x2, w_out_bf, g2, w_query_bf, sub_keys_bf, expert_up)


def _top_desc(s, k):
    out = []
    for _ in range(k):
        mx = jnp.max(s, axis=0, keepdims=True)
        out.append(mx)
        s = jnp.where(s == mx, -jnp.inf, s)
    return out


def _batcher_network(n):
    pairs, p = [], 1
    while p < n:
        k = p
        while k >= 1:
            for j in range(k % p, n - k, 2 * k):
                for i in range(min(k, n - j - k)):
                    if (i + j) // (2 * p) == (i + j + k) // (2 * p):
                        pairs.append((i + j, i + j + k))
            k //= 2
        p *= 2
    return pairs


ROUTE_SUBLANES = 8
ROUTE_LISTS = N_KEYS // ROUTE_SUBLANES
ROUTE_NET = _batcher_network(ROUTE_LISTS)
ROUTE_RANKS = PEER_TOPK + 1


def _top_desc_keys(s, k):
    lists = [s[i * ROUTE_SUBLANES:(i + 1) * ROUTE_SUBLANES] for i in range(ROUTE_LISTS)]
    for a, b in ROUTE_NET:
        lists[a], lists[b] = jnp.maximum(lists[a], lists[b]), jnp.minimum(lists[a], lists[b])
    lists.append(jnp.full_like(lists[0], -jnp.inf))
    out = []
    for t in range(k):
        mx = jnp.max(lists[0], axis=0, keepdims=True)
        out.append(mx)
        if t + 1 < k:
            hit = lists[0] == mx
            depth = min(k - 1 - t, ROUTE_LISTS)
            for d in range(depth):
                lists[d] = jnp.where(hit, lists[d + 1], lists[d])
    return out


def _route_kernel(s1_ref, s2_ref, thr_ref, p1_ref, p2_ref):
    pairs = [(i, j) for i in range(ROUTE_RANKS) for j in range(ROUTE_RANKS)
             if (i + 1) * (j + 1) <= ROUTE_RANKS]
    for h in range(PEER_HEADS):
        s1 = s1_ref[h]
        s2 = s2_ref[h]
        v1 = _top_desc_keys(s1, ROUTE_RANKS)
        v2 = _top_desc_keys(s2, ROUTE_RANKS)
        v1p = jnp.concatenate([v1[i] for i, _ in pairs], axis=0)
        v2p = jnp.concatenate([v2[j] for _, j in pairs], axis=0)
        cand = v1p + v2p
        top = _top_desc(cand, ROUTE_RANKS)
        tau = 0.5 * (top[PEER_TOPK - 1] + top[PEER_TOPK])
        m1, m2 = v1[0], v2[0]
        sel = jnp.exp(v2p - m2) >= jnp.exp((tau - v1p) - m2)
        z = jnp.sum(jnp.where(sel, jnp.exp(cand - (m1 + m2)), 0.0), axis=0, keepdims=True)
        pthr = jnp.exp((tau - s1) - m2)
        p1 = 0.5 * (jnp.exp(s1 - m1) / z)
        for c in range(pthr.shape[1] // 128):
            thr_ref[c, h] = pthr[:, c * 128:(c + 1) * 128]
            p1_ref[c, h] = p1[:, c * 128:(c + 1) * 128]
        p2_ref[h] = jnp.exp(s2 - m2)


def _route(s1t, s2t, tr):
    T = s1t.shape[-1]
    st = pl.BlockSpec((PEER_HEADS, N_KEYS, tr), lambda i: (0, 0, i))
    full = jax.ShapeDtypeStruct((PEER_HEADS, N_KEYS, T), F32)
    chunked = jax.ShapeDtypeStruct((T // 128, PEER_HEADS, N_KEYS, 128), F32)
    ct = pl.BlockSpec((tr // 128, PEER_HEADS, N_KEYS, 128), lambda i: (i, 0, 0, 0))
    return pl.pallas_call(
        _route_kernel,
        grid=(T // tr,),
        in_specs=[st, st],
        out_specs=[ct, ct, st],
        out_shape=[chunked, chunked, full],
        compiler_params=pltpu.CompilerParams(
            dimension_semantics=("arbitrary",), vmem_limit_bytes=VMEM_LIMIT_BYTES),
        name="route",
    )(s1t, s2t)


PEER_SUBLANES = 8
PEER_MXU_COLS = 256
PEER_KEY_ROWS = 16
PEER_VMEM_LIMIT_BYTES = 61 * 1024 * 1024


def _peer_order(n_chunks):
    return [(stage, qc) for qc in range(n_chunks) for stage in (3, 2, 1)]


def _peer_kernel(h2t_ref, dn_ref, upt_ref, thr_ref, p1_ref, p2_ref, o_ref,
                 a_scr, c_scr, *, rows_e1, n_etiles, n_tiles):
    s = pl.program_id(0)
    tm = o_ref.shape[1]

    @pl.when(s == 0)
    def _():
        a_scr[...] = jnp.zeros_like(a_scr)
        c_scr[...] = jnp.zeros_like(c_scr)

    @pl.when((s == 0) | ((s - 2) % n_etiles == 0))
    def _():
        o_ref[...] = jnp.zeros_like(o_ref)

    t2 = jnp.clip(s - 1, 0, n_tiles - 1)
    row0 = ((t2 % n_etiles) * rows_e1) % PEER_SUBLANES

    def stage3(qc):
        qsl = slice(qc * PEER_MXU_COLS, (qc + 1) * PEER_MXU_COLS)
        o_ref[:, qsl] += jnp.dot(upt_ref[...], c_scr[:, qsl], preferred_element_type=F32)

    def stage2(qc):
        for lc in range(PEER_MXU_COLS // 128):
            lcg = qc * (PEER_MXU_COLS // 128) + lc
            lsl = slice(lcg * 128, (lcg + 1) * 128)
            for k0 in range(0, N_KEYS, PEER_KEY_ROWS):
                ksl = slice(k0, k0 + PEER_KEY_ROWS)
                w = [jnp.zeros((PEER_KEY_ROWS, 128), F32) for _ in range(rows_e1)]
                for h in range(PEER_HEADS):
                    p2 = p2_ref[h, ksl, lsl]
                    for r in range(rows_e1):
                        bcast = pl.ds(row0 + r, PEER_KEY_ROWS, stride=0)
                        w[r] = w[r] + jnp.where(p2 >= thr_ref[lcg, h, bcast, :],
                                                p1_ref[lcg, h, bcast, :] * p2, 0.0)
                for r in range(rows_e1):
                    esl = slice(r * N_KEYS + k0, r * N_KEYS + k0 + PEER_KEY_ROWS)
                    a = a_scr[esl, lsl]
                    c_scr[esl, lsl] = (w[r] * (a * (1.0 + lax.erf(a * np_sqrt_half)))).astype(BF16)

    def stage1(qc):
        qsl = slice(qc * PEER_MXU_COLS, (qc + 1) * PEER_MXU_COLS)
        a_scr[:, qsl] = jnp.dot(dn_ref[...], h2t_ref[:, qsl], preferred_element_type=F32)

    stages = {1: stage1, 2: stage2, 3: stage3}
    for stage, qc in _peer_order(tm // PEER_MXU_COLS):
        stages[stage](qc)


def _peer(h2t, dn_bf, upt_bf, thr, p1t, p2t, tm, rows_e1):
    T = h2t.shape[1]
    E = rows_e1 * N_KEYS
    n_etiles = N_EXPERTS // E
    n_tiles = (T // tm) * n_etiles
    assert tm % PEER_MXU_COLS == 0 and PEER_SUBLANES % rows_e1 == 0

    def tile(delay):
        def f(s):
            t = jnp.clip(s - delay, 0, n_tiles - 1)
            return t // n_etiles, t % n_etiles
        return f

    st1, st2, st3 = tile(0), tile(1), tile(2)
    rows_blk = pl.BlockSpec(
        (tm // 128, PEER_HEADS, PEER_SUBLANES, 128),
        lambda s: (st2(s)[0], 0, (st2(s)[1] * rows_e1) // PEER_SUBLANES, 0))
    return pl.pallas_call(
        functools.partial(_peer_kernel, rows_e1=rows_e1, n_etiles=n_etiles, n_tiles=n_tiles),
        grid=(n_tiles + 2,),
        in_specs=[
            pl.BlockSpec((D_MODEL, tm), lambda s: (0, st1(s)[0])),
            pl.BlockSpec((E, D_MODEL), lambda s: (st1(s)[1], 0)),
            pl.BlockSpec((None, D_MODEL, E), lambda s: (st3(s)[1], 0, 0)),
            rows_blk, rows_blk,
            pl.BlockSpec((PEER_HEADS, N_KEYS, tm), lambda s: (0, 0, st2(s)[0])),
        ],
        out_specs=pl.BlockSpec((D_MODEL, tm), lambda s: (0, st3(s)[0])),
        out_shape=jax.ShapeDtypeStruct((D_MODEL, T), F32),
        scratch_shapes=[pltpu.VMEM((E, tm), F32), pltpu.VMEM((E, tm), BF16)],
        compiler_params=pltpu.CompilerParams(
            dimension_semantics=("arbitrary",), vmem_limit_bytes=PEER_VMEM_LIMIT_BYTES),
        name="peer",
    )(h2t, dn_bf, upt_bf, thr, p1t, p2t)


def _final_kernel(x1_ref, pt_ref, g_ref, o_ref):
    xo = x1_ref[...] + pt_ref[...].T
    o_ref[...] = _rmsnorm(xo, g_ref[...])


def _final(x1, peer_t, gf, tm):
    T = x1.shape[0]
    return pl.pallas_call(
        _final_kernel,
        grid=(T // tm,),
        in_specs=[pl.BlockSpec((tm, D_MODEL), lambda i: (i, 0)),
                  pl.BlockSpec((D_MODEL, tm), lambda i: (0, i)),
                  _resident((1, D_MODEL))],
        out_specs=pl.BlockSpec((tm, D_MODEL), lambda i: (i, 0)),
        out_shape=jax.ShapeDtypeStruct((T, D_MODEL), F32),
        compiler_params=pltpu.CompilerParams(
            dimension_semantics=("arbitrary",), vmem_limit_bytes=VMEM_LIMIT_BYTES),
        name="final",
    )(x1, peer_t, gf)


def kernel(x, norm1_g, w_in, sgu_ln_g, sgu_ln_b, w_spatial, b_spatial, attn_sinks, w_out,
           norm2_g, w_query, sub_keys, expert_down, expert_up, norm_f_g):
    B, S, D = x.shape
    assert D == D_MODEL and S % WINDOW == 0 and norm1_g.shape[0] == 1
    T = B * S
    nb = S // WINDOW
    x2 = x.reshape(T, D)

    rows_e1 = 8
    u, v, q, kv, dn_bf = _in_proj(
        x2, norm1_g[0].reshape(1, D), w_in[0].astype(BF16), expert_down[0], tm=512)

    bs_full = jnp.repeat(b_spatial[0].T, SGU_DIM, axis=1)
    y = _mix(attn_sinks[0], u, v, q, kv,
             sgu_ln_g[0].reshape(1, SGU_WIDTH), sgu_ln_b[0].reshape(1, SGU_WIDTH),
             w_spatial[0], bs_full, B, nb, nsub=1)

    x1, h2t, s1t, s2t, upt_bf = _out_proj(
        y, x2, w_out[0].astype(BF16), norm2_g[0].reshape(1, D), w_query[0].astype(BF16),
        sub_keys[0].astype(BF16), expert_up[0], tm=256, peer_tile=rows_e1 * N_KEYS)

    pthr, p1t, p2t = _route(s1t, s2t, tr=256)

    peer_t = _peer(h2t, dn_bf, upt_bf, pthr, p1t, p2t, tm=1024, rows_e1=rows_e1)

    out = _final(x1, peer_t, norm_f_g.reshape(1, D), tm=256)
    return out.reshape(B, S, D)
```

```python
import functools
import math

import jax
import jax.numpy as jnp
from jax import lax
from jax.experimental import pallas as pl
from jax.experimental.pallas import tpu as pltpu

D_MODEL = 2048
CHUNK = 128
SGU_GROUPS = 8
SGU_DIM = 128
SGU_WIDTH = SGU_GROUPS * SGU_DIM
HEAD_DIM = 64
N_Q_HEADS = 16
N_KV_HEADS = 2
Q_PER_KV = N_Q_HEADS // N_KV_HEADS
WINDOW = 128
ATTN_WIDTH = N_Q_HEADS * HEAD_DIM
KV_WIDTH = N_KV_HEADS * HEAD_DIM
IN_WIDTH = 2 * SGU_WIDTH + ATTN_WIDTH + 2 * KV_WIDTH
PEER_HEADS = 8
N_KEYS = 128
N_EXPERTS = N_KEYS * N_KEYS
PEER_TOPK = 16
D_QUERY = 256
EPS = 1e-6
ATTN_SCALE = HEAD_DIM ** -0.5

VMEM_LIMIT_BYTES = 56 * 1024 * 1024

IN_PROJ_ROWS = 512
OUT_PROJ_ROWS = 256
ROUTE_TOKENS = 256
PEER_TOKENS = 1024
PEER_ROWS_E1 = 8
FINAL_ROWS = 256

BF16 = jnp.bfloat16
F32 = jnp.float32


np_sqrt_half = math.sqrt(0.5)


def _gelu(x):
    return 0.5 * x * (1.0 + lax.erf(x * np_sqrt_half))


def _rmsnorm(x, g):
    return x * lax.rsqrt(jnp.mean(x * x, axis=-1, keepdims=True) + EPS) * g


def _resident(shape):
    nd = len(shape)
    return pl.BlockSpec(shape, lambda *_: (0,) * nd, pipeline_mode=pl.Buffered(1))


def _in_proj_kernel(x_ref, g_ref, w_ref, dn_ref, u_ref, v_ref, q_ref, kv_ref, dnb_ref):
    dnb_ref[...] = dn_ref[...].astype(BF16)
    h = _rmsnorm(x_ref[...], g_ref[...]).astype(BF16)
    o0, o1, o2, o3 = SGU_WIDTH, 2 * SGU_WIDTH, 2 * SGU_WIDTH + ATTN_WIDTH, IN_WIDTH
    zu = jnp.dot(h, w_ref[:, 0:o0], preferred_element_type=F32)
    u_ref[...] = _gelu(zu).astype(BF16)
    zv = jnp.dot(h, w_ref[:, o0:o1], preferred_element_type=F32)
    v_ref[...] = _gelu(zv).astype(BF16)
    q_ref[...] = jnp.dot(h, w_ref[:, o1:o2], preferred_element_type=F32).astype(BF16)
    kv_ref[...] = jnp.dot(h, w_ref[:, o2:o3], preferred_element_type=F32).astype(BF16)


def _expert_slab(steps):
    assert N_EXPERTS % steps == 0 and (N_EXPERTS // steps) % 128 == 0
    return N_EXPERTS // steps


def _in_proj(x2, g1, w_in_bf, expert_down, tm):
    T = x2.shape[0]
    steps = T // tm
    row = lambda w: pl.BlockSpec((tm, w), lambda i: (i, 0))
    slab_spec = pl.BlockSpec((_expert_slab(steps), D_MODEL), lambda i: (i, 0))
    return pl.pallas_call(
        _in_proj_kernel,
        grid=(steps,),
        in_specs=[row(D_MODEL), _resident((1, D_MODEL)), _resident((D_MODEL, IN_WIDTH)),
                  slab_spec],
        out_specs=[row(SGU_WIDTH), row(SGU_WIDTH), row(ATTN_WIDTH), row(2 * KV_WIDTH),
                   slab_spec],
        out_shape=[
            jax.ShapeDtypeStruct((T, SGU_WIDTH), BF16),
            jax.ShapeDtypeStruct((T, SGU_WIDTH), BF16),
            jax.ShapeDtypeStruct((T, ATTN_WIDTH), BF16),
            jax.ShapeDtypeStruct((T, 2 * KV_WIDTH), BF16),
            jax.ShapeDtypeStruct((N_EXPERTS, D_MODEL), BF16),
        ],
        compiler_params=pltpu.CompilerParams(
            dimension_semantics=("arbitrary",), vmem_limit_bytes=VMEM_LIMIT_BYTES),
        name="in_proj",
    )(x2, g1, w_in_bf, expert_down)


def _mix_kernel(sink_ref, u_ref, v_ref, q_ref, kv_ref, kvp_ref, lng_ref, lnb_ref,
                ws_ref, bs_ref, y_ref):
    n = pl.program_id(1)
    row_t = lax.broadcasted_iota(jnp.int32, (CHUNK, CHUNK), 0)
    col_s = lax.broadcasted_iota(jnp.int32, (CHUNK, CHUNK), 1)
    causal = col_s <= row_t
    for g in range(SGU_GROUPS):
        sl = slice(g * SGU_DIM, (g + 1) * SGU_DIM)
        vg = v_ref[:, sl].astype(F32)
        mu = jnp.mean(vg, axis=-1, keepdims=True)
        var = jnp.mean(jnp.square(vg - mu), axis=-1, keepdims=True)
        vn = ((vg - mu) * lax.rsqrt(var + EPS) * lng_ref[:, sl] + lnb_ref[:, sl]).astype(BF16)
        w = jnp.where(causal, ws_ref[g], 0.0).astype(BF16)
        s = jnp.dot(w, vn, preferred_element_type=F32) + bs_ref[:, sl]
        y_ref[:, sl] = (u_ref[:, sl].astype(F32) * s).astype(BF16)

    qi = lax.broadcasted_iota(jnp.int32, (WINDOW, 2 * WINDOW), 0)
    kj = lax.broadcasted_iota(jnp.int32, (WINDOW, 2 * WINDOW), 1)
    band = (kj >= qi + 1) & (kj <= qi + WINDOW)
    mask = band & ((n > 0) | (kj >= WINDOW))
    for j in range(N_KV_HEADS):
        ksl = slice(j * HEAD_DIM, (j + 1) * HEAD_DIM)
        vsl = slice(KV_WIDTH + j * HEAD_DIM, KV_WIDTH + (j + 1) * HEAD_DIM)
        kk = jnp.concatenate([kvp_ref[:, ksl], kv_ref[:, ksl]], axis=0)
        vv = jnp.concatenate([kvp_ref[:, vsl], kv_ref[:, vsl]], axis=0)
        for gq in range(Q_PER_KV):
            hq = j * Q_PER_KV + gq
            qh = q_ref[:, hq * HEAD_DIM:(hq + 1) * HEAD_DIM]
            sc = lax.dot_general(qh, kk, (((1,), (1,)), ((), ())),
                                 preferred_element_type=F32) * ATTN_SCALE
            sc = jnp.where(mask, sc, -jnp.inf)
            sink = sink_ref[hq]
            m = jnp.maximum(jnp.max(sc, axis=-1, keepdims=True), sink)
            p = jnp.exp(sc - m)
            denom = jnp.sum(p, axis=-1, keepdims=True) + jnp.exp(sink - m)
            probs = (p / denom).astype(BF16)
            oh = jnp.dot(probs, vv, preferred_element_type=F32)
            c0 = SGU_WIDTH + hq * HEAD_DIM
            y_ref[:, c0:c0 + HEAD_DIM] = oh.astype(BF16)


def _mix(sinks, u, v, q, kv, lng, lnb, ws, bs_full, batch, nb):
    T = u.shape[0]
    cur = lambda w: pl.BlockSpec((WINDOW, w), lambda b, n: (b * nb + n, 0))
    prev = pl.BlockSpec((WINDOW, 2 * KV_WIDTH),
                        lambda b, n: (b * nb + jnp.maximum(n - 1, 0), 0))
    return pl.pallas_call(
        _mix_kernel,
        grid=(batch, nb),
        in_specs=[
            pl.BlockSpec(memory_space=pltpu.SMEM),
            cur(SGU_WIDTH), cur(SGU_WIDTH), cur(ATTN_WIDTH), cur(2 * KV_WIDTH), prev,
            _resident((1, SGU_WIDTH)), _resident((1, SGU_WIDTH)),
            _resident((SGU_GROUPS, CHUNK, CHUNK)), _resident((CHUNK, SGU_WIDTH)),
        ],
        out_specs=cur(D_MODEL),
        out_shape=jax.ShapeDtypeStruct((T, D_MODEL), BF16),
        compiler_params=pltpu.CompilerParams(
            dimension_semantics=("arbitrary", "arbitrary"), vmem_limit_bytes=VMEM_LIMIT_BYTES),
        name="mix",
    )(sinks, u, v, q, kv, kv, lng, lnb, ws, bs_full)


def _out_proj_kernel(y_ref, x_ref, wo_ref, g2_ref, wq_ref, sk_ref, up_ref,
                     x1_ref, h2t_ref, s1_ref, s2_ref, upt_ref):
    upt_ref[...] = up_ref[...].astype(BF16).T
    x1 = x_ref[...] + jnp.dot(y_ref[...], wo_ref[...], preferred_element_type=F32)
    x1_ref[...] = x1
    h2 = _rmsnorm(x1, g2_ref[...]).astype(BF16)
    h2t_ref[...] = h2.T
    qry = jnp.dot(h2, wq_ref[...], preferred_element_type=F32).astype(BF16)
    half = D_QUERY // 2
    for h in range(PEER_HEADS):
        for c, s_ref in ((0, s1_ref), (1, s2_ref)):
            qc = qry[:, h * D_QUERY + c * half: h * D_QUERY + (c + 1) * half]
            s_ref[h] = lax.dot_general(sk_ref[c], qc, (((1,), (1,)), ((), ())),
                                       preferred_element_type=F32)


def _out_proj(y, x2, w_out_bf, g2, w_query_bf, sub_keys_bf, expert_up, tm, peer_tile):
    T = x2.shape[0]
    steps = T // tm
    slab = _expert_slab(steps)
    assert peer_tile % slab == 0
    per_tile = peer_tile // slab
    row = lambda w: pl.BlockSpec((tm, w), lambda i: (i, 0))
    st = pl.BlockSpec((PEER_HEADS, N_KEYS, tm), lambda i: (0, 0, i))
    return pl.pallas_call(
        _out_proj_kernel,
        grid=(steps,),
        in_specs=[row(D_MODEL), row(D_MODEL), _resident((D_MODEL, D_MODEL)),
                  _resident((1, D_MODEL)), _resident((D_MODEL, PEER_HEADS * D_QUERY)),
                  _resident((2, N_KEYS, D_QUERY // 2)),
                  pl.BlockSpec((slab, D_MODEL), lambda i: (i, 0))],
        out_specs=[row(D_MODEL), pl.BlockSpec((D_MODEL, tm), lambda i: (0, i)), st, st,
                   pl.BlockSpec((None, D_MODEL, slab), lambda i: (i // per_tile, 0, i % per_tile))],
        out_shape=[
            jax.ShapeDtypeStruct((T, D_MODEL), F32),
            jax.ShapeDtypeStruct((D_MODEL, T), BF16),
            jax.ShapeDtypeStruct((PEER_HEADS, N_KEYS, T), F32),
            jax.ShapeDtypeStruct((PEER_HEADS, N_KEYS, T), F32),
            jax.ShapeDtypeStruct((N_EXPERTS // peer_tile, D_MODEL, peer_tile), BF16),
        ],
        compiler_params=pltpu.CompilerParams(
            dimension_semantics=("arbitrary",), vmem_limit_bytes=VMEM_LIMIT_BYTES),
        name="out_proj",
    )(y, x2, w_out_bf, g2, w_query_bf, sub_keys_bf, expert_up)


def _top_desc(s, k):
    out = []
    for _ in range(k):
        mx = jnp.max(s, axis=0, keepdims=True)
        out.append(mx)
        s = jnp.where(s == mx, -jnp.inf, s)
    return out


def _batcher_network(n):
    pairs, p = [], 1
    while p < n:
        k = p
        while k >= 1:
            for j in range(k % p, n - k, 2 * k):
                for i in range(min(k, n - j - k)):
                    if (i + j) // (2 * p) == (i + j + k) // (2 * p):
                        pairs.append((i + j, i + j + k))
            k //= 2
        p *= 2
    return pairs


ROUTE_SUBLANES = 8
ROUTE_LISTS = N_KEYS // ROUTE_SUBLANES
ROUTE_NET = _batcher_network(ROUTE_LISTS)
ROUTE_RANKS = PEER_TOPK + 1


def _top_desc_keys(s, k):
    lists = [s[i * ROUTE_SUBLANES:(i + 1) * ROUTE_SUBLANES] for i in range(ROUTE_LISTS)]
    for a, b in ROUTE_NET:
        lists[a], lists[b] = jnp.maximum(lists[a], lists[b]), jnp.minimum(lists[a], lists[b])
    lists.append(jnp.full_like(lists[0], -jnp.inf))
    out = []
    for t in range(k):
        mx = jnp.max(lists[0], axis=0, keepdims=True)
        out.append(mx)
        if t + 1 < k:
            hit = lists[0] == mx
            depth = min(k - 1 - t, ROUTE_LISTS)
            for d in range(depth):
                lists[d] = jnp.where(hit, lists[d + 1], lists[d])
    return out


def _route_kernel(s1_ref, s2_ref, thr_ref, p1_ref, p2_ref):
    pairs = [(i, j) for i in range(ROUTE_RANKS) for j in range(ROUTE_RANKS)
             if (i + 1) * (j + 1) <= ROUTE_RANKS]
    for h in range(PEER_HEADS):
        s1 = s1_ref[h]
        s2 = s2_ref[h]
        v1 = _top_desc_keys(s1, ROUTE_RANKS)
        v2 = _top_desc_keys(s2, ROUTE_RANKS)
        v1p = jnp.concatenate([v1[i] for i, _ in pairs], axis=0)
        v2p = jnp.concatenate([v2[j] for _, j in pairs], axis=0)
        cand = v1p + v2p
        top = _top_desc(cand, ROUTE_RANKS)
        tau = 0.5 * (top[PEER_TOPK - 1] + top[PEER_TOPK])
        m1, m2 = v1[0], v2[0]
        sel = jnp.exp(v2p - m2) >= jnp.exp((tau - v1p) - m2)
        z = jnp.sum(jnp.where(sel, jnp.exp(cand - (m1 + m2)), 0.0), axis=0, keepdims=True)
        pthr = jnp.exp((tau - s1) - m2)
        p1 = 0.5 * (jnp.exp(s1 - m1) / z)
        for c in range(pthr.shape[1] // 128):
            thr_ref[c, h] = pthr[:, c * 128:(c + 1) * 128]
            p1_ref[c, h] = p1[:, c * 128:(c + 1) * 128]
        p2_ref[h] = jnp.exp(s2 - m2)


def _route(s1t, s2t, tr):
    T = s1t.shape[-1]
    st = pl.BlockSpec((PEER_HEADS, N_KEYS, tr), lambda i: (0, 0, i))
    full = jax.ShapeDtypeStruct((PEER_HEADS, N_KEYS, T), F32)
    chunked = jax.ShapeDtypeStruct((T // 128, PEER_HEADS, N_KEYS, 128), F32)
    ct = pl.BlockSpec((tr // 128, PEER_HEADS, N_KEYS, 128), lambda i: (i, 0, 0, 0))
    return pl.pallas_call(
        _route_kernel,
        grid=(T // tr,),
        in_specs=[st, st],
        out_specs=[ct, ct, st],
        out_shape=[chunked, chunked, full],
        compiler_params=pltpu.CompilerParams(
            dimension_semantics=("arbitrary",), vmem_limit_bytes=VMEM_LIMIT_BYTES),
        name="route",
    )(s1t, s2t)


PEER_SUBLANES = 8
PEER_MXU_COLS = 256
PEER_KEY_ROWS = 16
PEER_VMEM_LIMIT_BYTES = 61 * 1024 * 1024


def _peer_order(n_chunks):
    return [(stage, qc) for qc in range(n_chunks) for stage in (3, 2, 1)]


def _peer_kernel(h2t_ref, dn_ref, upt_ref, thr_ref, p1_ref, p2_ref, o_ref,
                 a_scr, c_scr, *, rows_e1, n_etiles, n_tiles):
    s = pl.program_id(0)
    tm = o_ref.shape[1]

    @pl.when(s == 0)
    def _():
        a_scr[...] = jnp.zeros_like(a_scr)
        c_scr[...] = jnp.zeros_like(c_scr)

    @pl.when((s == 0) | ((s - 2) % n_etiles == 0))
    def _():
        o_ref[...] = jnp.zeros_like(o_ref)

    t2 = jnp.clip(s - 1, 0, n_tiles - 1)
    row0 = ((t2 % n_etiles) * rows_e1) % PEER_SUBLANES

    def stage3(qc):
        qsl = slice(qc * PEER_MXU_COLS, (qc + 1) * PEER_MXU_COLS)
        o_ref[:, qsl] += jnp.dot(upt_ref[...], c_scr[:, qsl], preferred_element_type=F32)

    def stage2(qc):
        for lc in range(PEER_MXU_COLS // 128):
            lcg = qc * (PEER_MXU_COLS // 128) + lc
            lsl = slice(lcg * 128, (lcg + 1) * 128)
            for k0 in range(0, N_KEYS, PEER_KEY_ROWS):
                ksl = slice(k0, k0 + PEER_KEY_ROWS)
                w = [jnp.zeros((PEER_KEY_ROWS, 128), F32) for _ in range(rows_e1)]
                for h in range(PEER_HEADS):
                    p2 = p2_ref[h, ksl, lsl]
                    for r in range(rows_e1):
                        bcast = pl.ds(row0 + r, PEER_KEY_ROWS, stride=0)
                        w[r] = w[r] + jnp.where(p2 >= thr_ref[lcg, h, bcast, :],
                                                p1_ref[lcg, h, bcast, :] * p2, 0.0)
                for r in range(rows_e1):
                    esl = slice(r * N_KEYS + k0, r * N_KEYS + k0 + PEER_KEY_ROWS)
                    a = a_scr[esl, lsl]
                    c_scr[esl, lsl] = (w[r] * (a * (1.0 + lax.erf(a * np_sqrt_half)))).astype(BF16)

    def stage1(qc):
        qsl = slice(qc * PEER_MXU_COLS, (qc + 1) * PEER_MXU_COLS)
        a_scr[:, qsl] = jnp.dot(dn_ref[...], h2t_ref[:, qsl], preferred_element_type=F32)

    stages = {1: stage1, 2: stage2, 3: stage3}
    for stage, qc in _peer_order(tm // PEER_MXU_COLS):
        stages[stage](qc)


def _peer(h2t, dn_bf, upt_bf, thr, p1t, p2t, tm, rows_e1):
    T = h2t.shape[1]
    E = rows_e1 * N_KEYS
    n_etiles = N_EXPERTS // E
    n_tiles = (T // tm) * n_etiles
    assert tm % PEER_MXU_COLS == 0 and PEER_SUBLANES % rows_e1 == 0

    def tile(delay):
        def f(s):
            t = jnp.clip(s - delay, 0, n_tiles - 1)
            return t // n_etiles, t % n_etiles
        return f

    st1, st2, st3 = tile(0), tile(1), tile(2)
    rows_blk = pl.BlockSpec(
        (tm // 128, PEER_HEADS, PEER_SUBLANES, 128),
        lambda s: (st2(s)[0], 0, (st2(s)[1] * rows_e1) // PEER_SUBLANES, 0))
    return pl.pallas_call(
        functools.partial(_peer_kernel, rows_e1=rows_e1, n_etiles=n_etiles, n_tiles=n_tiles),
        grid=(n_tiles + 2,),
        in_specs=[
            pl.BlockSpec((D_MODEL, tm), lambda s: (0, st1(s)[0])),
            pl.BlockSpec((E, D_MODEL), lambda s: (st1(s)[1], 0)),
            pl.BlockSpec((None, D_MODEL, E), lambda s: (st3(s)[1], 0, 0)),
            rows_blk, rows_blk,
            pl.BlockSpec((PEER_HEADS, N_KEYS, tm), lambda s: (0, 0, st2(s)[0])),
        ],
        out_specs=pl.BlockSpec((D_MODEL, tm), lambda s: (0, st3(s)[0])),
        out_shape=jax.ShapeDtypeStruct((D_MODEL, T), F32),
        scratch_shapes=[pltpu.VMEM((E, tm), F32), pltpu.VMEM((E, tm), BF16)],
        compiler_params=pltpu.CompilerParams(
            dimension_semantics=("arbitrary",), vmem_limit_bytes=PEER_VMEM_LIMIT_BYTES),
        name="peer",
    )(h2t, dn_bf, upt_bf, thr, p1t, p2t)


def _final_kernel(x1_ref, pt_ref, g_ref, o_ref):
    xo = x1_ref[...] + pt_ref[...].T
    o_ref[...] = _rmsnorm(xo, g_ref[...])


def _final(x1, peer_t, gf, tm):
    T = x1.shape[0]
    return pl.pallas_call(
        _final_kernel,
        grid=(T // tm,),
        in_specs=[pl.BlockSpec((tm, D_MODEL), lambda i: (i, 0)),
                  pl.BlockSpec((D_MODEL, tm), lambda i: (0, i)),
                  _resident((1, D_MODEL))],
        out_specs=pl.BlockSpec((tm, D_MODEL), lambda i: (i, 0)),
        out_shape=jax.ShapeDtypeStruct((T, D_MODEL), F32),
        compiler_params=pltpu.CompilerParams(
            dimension_semantics=("arbitrary",), vmem_limit_bytes=VMEM_LIMIT_BYTES),
        name="final",
    )(x1, peer_t, gf)


def kernel(x, norm1_g, w_in, sgu_ln_g, sgu_ln_b, w_spatial, b_spatial, attn_sinks, w_out,
           norm2_g, w_query, sub_keys, expert_down, expert_up, norm_f_g):
    B, S, D = x.shape
    assert D == D_MODEL and S % WINDOW == 0 and norm1_g.shape[0] == 1
    T = B * S
    nb = S // WINDOW
    x2 = x.reshape(T, D)

    u, v, q, kv, dn_bf = _in_proj(
        x2, norm1_g[0].reshape(1, D), w_in[0].astype(BF16), expert_down[0], tm=IN_PROJ_ROWS)

    bs_full = jnp.repeat(b_spatial[0].T, SGU_DIM, axis=1)
    y = _mix(attn_sinks[0], u, v, q, kv,
             sgu_ln_g[0].reshape(1, SGU_WIDTH), sgu_ln_b[0].reshape(1, SGU_WIDTH),
             w_spatial[0], bs_full, B, nb)

    x1, h2t, s1t, s2t, upt_bf = _out_proj(
        y, x2, w_out[0].astype(BF16), norm2_g[0].reshape(1, D), w_query[0].astype(BF16),
        sub_keys[0].astype(BF16), expert_up[0], tm=OUT_PROJ_ROWS,
        peer_tile=PEER_ROWS_E1 * N_KEYS)

    pthr, p1t, p2t = _route(s1t, s2t, tr=ROUTE_TOKENS)

    peer_t = _peer(h2t, dn_bf, upt_bf, pthr, p1t, p2t, tm=PEER_TOKENS, rows_e1=PEER_ROWS_E1)

    out = _final(x1, peer_t, norm_f_g.reshape(1, D), tm=FINAL_ROWS)
    return out.reshape(B, S, D)
```

```python
import functools
import math

import jax
import jax.numpy as jnp
from jax import lax
from jax.experimental import pallas as pl
from jax.experimental.pallas import tpu as pltpu

D_MODEL = 2048
CHUNK = 128
SGU_GROUPS = 8
SGU_DIM = 128
SGU_WIDTH = SGU_GROUPS * SGU_DIM
HEAD_DIM = 64
N_Q_HEADS = 16
N_KV_HEADS = 2
Q_PER_KV = N_Q_HEADS // N_KV_HEADS
WINDOW = 128
ATTN_WIDTH = N_Q_HEADS * HEAD_DIM
KV_WIDTH = N_KV_HEADS * HEAD_DIM
IN_WIDTH = 2 * SGU_WIDTH + ATTN_WIDTH + 2 * KV_WIDTH
PEER_HEADS = 8
N_KEYS = 128
N_EXPERTS = N_KEYS * N_KEYS
PEER_TOPK = 16
D_QUERY = 256
EPS = 1e-6
ATTN_SCALE = HEAD_DIM ** -0.5

VMEM_LIMIT_BYTES = 56 * 1024 * 1024

IN_PROJ_ROWS = 512
OUT_PROJ_ROWS = 256
ROUTE_TOKENS = 256
PEER_TOKENS = 1024
PEER_ROWS_E1 = 8
FINAL_ROWS = 256

BF16 = jnp.bfloat16
F32 = jnp.float32


np_sqrt_half = math.sqrt(0.5)


def _gelu(x):
    return 0.5 * x * (1.0 + lax.erf(x * np_sqrt_half))


def _rmsnorm(x, g):
    return x * lax.rsqrt(jnp.mean(x * x, axis=-1, keepdims=True) + EPS) * g


def _resident(shape):
    nd = len(shape)
    return pl.BlockSpec(shape, lambda *_: (0,) * nd, pipeline_mode=pl.Buffered(1))


def _in_proj_kernel(x_ref, g_ref, w_ref, dn_ref, u_ref, v_ref, q_ref, kv_ref, dnb_ref):
    dnb_ref[...] = dn_ref[...].astype(BF16)
    h = _rmsnorm(x_ref[...], g_ref[...]).astype(BF16)
    o0, o1, o2, o3 = SGU_WIDTH, 2 * SGU_WIDTH, 2 * SGU_WIDTH + ATTN_WIDTH, IN_WIDTH
    zu = jnp.dot(h, w_ref[:, 0:o0], preferred_element_type=F32)
    u_ref[...] = _gelu(zu).astype(BF16)
    zv = jnp.dot(h, w_ref[:, o0:o1], preferred_element_type=F32)
    v_ref[...] = _gelu(zv).astype(BF16)
    q_ref[...] = jnp.dot(h, w_ref[:, o1:o2], preferred_element_type=F32).astype(BF16)
    kv_ref[...] = jnp.dot(h, w_ref[:, o2:o3], preferred_element_type=F32).astype(BF16)


def _expert_slab(steps):
    assert N_EXPERTS % steps == 0 and (N_EXPERTS // steps) % 128 == 0
    return N_EXPERTS // steps


def _in_proj(x2, g1, w_in_bf, expert_down, tm):
    T = x2.shape[0]
    steps = T // tm
    row = lambda w: pl.BlockSpec((tm, w), lambda i: (i, 0))
    slab_spec = pl.BlockSpec((_expert_slab(steps), D_MODEL), lambda i: (i, 0))
    return pl.pallas_call(
        _in_proj_kernel,
        grid=(steps,),
        in_specs=[row(D_MODEL), _resident((1, D_MODEL)), _resident((D_MODEL, IN_WIDTH)),
                  slab_spec],
        out_specs=[row(SGU_WIDTH), row(SGU_WIDTH), row(ATTN_WIDTH), row(2 * KV_WIDTH),
                   slab_spec],
        out_shape=[
            jax.ShapeDtypeStruct((T, SGU_WIDTH), BF16),
            jax.ShapeDtypeStruct((T, SGU_WIDTH), BF16),
            jax.ShapeDtypeStruct((T, ATTN_WIDTH), BF16),
            jax.ShapeDtypeStruct((T, 2 * KV_WIDTH), BF16),
            jax.ShapeDtypeStruct((N_EXPERTS, D_MODEL), BF16),
        ],
        compiler_params=pltpu.CompilerParams(
            dimension_semantics=("arbitrary",), vmem_limit_bytes=VMEM_LIMIT_BYTES),
        name="in_proj",
    )(x2, g1, w_in_bf, expert_down)


def _mix_kernel(sink_ref, u_ref, v_ref, q_ref, kv_ref, kvp_ref, lng_ref, lnb_ref,
                ws_ref, bs_ref, wo_ref, wq_ref, y_ref, wob_ref, wqb_ref):
    n = pl.program_id(1)
    wob_ref[...] = wo_ref[...].astype(BF16)
    wqb_ref[...] = wq_ref[...].astype(BF16)
    row_t = lax.broadcasted_iota(jnp.int32, (CHUNK, CHUNK), 0)
    col_s = lax.broadcasted_iota(jnp.int32, (CHUNK, CHUNK), 1)
    causal = col_s <= row_t
    for g in range(SGU_GROUPS):
        sl = slice(g * SGU_DIM, (g + 1) * SGU_DIM)
        vg = v_ref[:, sl].astype(F32)
        mu = jnp.mean(vg, axis=-1, keepdims=True)
        var = jnp.mean(jnp.square(vg - mu), axis=-1, keepdims=True)
        vn = ((vg - mu) * lax.rsqrt(var + EPS) * lng_ref[:, sl] + lnb_ref[:, sl]).astype(BF16)
        w = jnp.where(causal, ws_ref[g], 0.0).astype(BF16)
        s = jnp.dot(w, vn, preferred_element_type=F32) + bs_ref[:, sl]
        y_ref[:, sl] = (u_ref[:, sl].astype(F32) * s).astype(BF16)

    qi = lax.broadcasted_iota(jnp.int32, (WINDOW, 2 * WINDOW), 0)
    kj = lax.broadcasted_iota(jnp.int32, (WINDOW, 2 * WINDOW), 1)
    band = (kj >= qi + 1) & (kj <= qi + WINDOW)
    mask = band & ((n > 0) | (kj >= WINDOW))
    for j in range(N_KV_HEADS):
        ksl = slice(j * HEAD_DIM, (j + 1) * HEAD_DIM)
        vsl = slice(KV_WIDTH + j * HEAD_DIM, KV_WIDTH + (j + 1) * HEAD_DIM)
        kk = jnp.concatenate([kvp_ref[:, ksl], kv_ref[:, ksl]], axis=0)
        vv = jnp.concatenate([kvp_ref[:, vsl], kv_ref[:, vsl]], axis=0)
        for gq in range(Q_PER_KV):
            hq = j * Q_PER_KV + gq
            qh = q_ref[:, hq * HEAD_DIM:(hq + 1) * HEAD_DIM]
            sc = lax.dot_general(qh, kk, (((1,), (1,)), ((), ())),
                                 preferred_element_type=F32) * ATTN_SCALE
            sc = jnp.where(mask, sc, -jnp.inf)
            sink = sink_ref[hq]
            m = jnp.maximum(jnp.max(sc, axis=-1, keepdims=True), sink)
            p = jnp.exp(sc - m)
            denom = jnp.sum(p, axis=-1, keepdims=True) + jnp.exp(sink - m)
            probs = (p / denom).astype(BF16)
            oh = jnp.dot(probs, vv, preferred_element_type=F32)
            c0 = SGU_WIDTH + hq * HEAD_DIM
            y_ref[:, c0:c0 + HEAD_DIM] = oh.astype(BF16)


def _mix(sinks, u, v, q, kv, lng, lnb, ws, bs_full, w_out, w_query, batch, nb):
    T = u.shape[0]
    steps = batch * nb
    assert w_out.shape == w_query.shape == (D_MODEL, D_MODEL)
    assert D_MODEL % steps == 0 and (D_MODEL // steps) % 16 == 0
    wrows = D_MODEL // steps
    cur = lambda w: pl.BlockSpec((WINDOW, w), lambda b, n: (b * nb + n, 0))
    prev = pl.BlockSpec((WINDOW, 2 * KV_WIDTH),
                        lambda b, n: (b * nb + jnp.maximum(n - 1, 0), 0))
    wslab = pl.BlockSpec((wrows, D_MODEL), lambda b, n: (b * nb + n, 0))
    return pl.pallas_call(
        _mix_kernel,
        grid=(batch, nb),
        in_specs=[
            pl.BlockSpec(memory_space=pltpu.SMEM),
            cur(SGU_WIDTH), cur(SGU_WIDTH), cur(ATTN_WIDTH), cur(2 * KV_WIDTH), prev,
            _resident((1, SGU_WIDTH)), _resident((1, SGU_WIDTH)),
            _resident((SGU_GROUPS, CHUNK, CHUNK)), _resident((CHUNK, SGU_WIDTH)),
            wslab, wslab,
        ],
        out_specs=[cur(D_MODEL), wslab, wslab],
        out_shape=[jax.ShapeDtypeStruct((T, D_MODEL), BF16),
                   jax.ShapeDtypeStruct((D_MODEL, D_MODEL), BF16),
                   jax.ShapeDtypeStruct((D_MODEL, PEER_HEADS * D_QUERY), BF16)],
        compiler_params=pltpu.CompilerParams(
            dimension_semantics=("arbitrary", "arbitrary"), vmem_limit_bytes=VMEM_LIMIT_BYTES),
        name="mix",
    )(sinks, u, v, q, kv, kv, lng, lnb, ws, bs_full, w_out, w_query)


def _out_proj_kernel(y_ref, x_ref, wo_ref, g2_ref, wq_ref, sk_ref, up_ref,
                     x1_ref, h2t_ref, s1_ref, s2_ref, upt_ref):
    upt_ref[...] = up_ref[...].astype(BF16).T
    x1 = x_ref[...] + jnp.dot(y_ref[...], wo_ref[...], preferred_element_type=F32)
    x1_ref[...] = x1
    h2 = _rmsnorm(x1, g2_ref[...]).astype(BF16)
    h2t_ref[...] = h2.T
    qry = jnp.dot(h2, wq_ref[...], preferred_element_type=F32).astype(BF16)
    half = D_QUERY // 2
    for h in range(PEER_HEADS):
        for c, s_ref in ((0, s1_ref), (1, s2_ref)):
            qc = qry[:, h * D_QUERY + c * half: h * D_QUERY + (c + 1) * half]
            s_ref[h] = lax.dot_general(sk_ref[c], qc, (((1,), (1,)), ((), ())),
                                       preferred_element_type=F32)


def _out_proj(y, x2, w_out_bf, g2, w_query_bf, sub_keys_bf, expert_up, tm, peer_tile):
    T = x2.shape[0]
    steps = T // tm
    slab = _expert_slab(steps)
    assert peer_tile % slab == 0
    per_tile = peer_tile // slab
    row = lambda w: pl.BlockSpec((tm, w), lambda i: (i, 0))
    st = pl.BlockSpec((PEER_HEADS, N_KEYS, tm), lambda i: (0, 0, i))
    return pl.pallas_call(
        _out_proj_kernel,
        grid=(steps,),
        in_specs=[row(D_MODEL), row(D_MODEL), _resident((D_MODEL, D_MODEL)),
                  _resident((1, D_MODEL)), _resident((D_MODEL, PEER_HEADS * D_QUERY)),
                  _resident((2, N_KEYS, D_QUERY // 2)),
                  pl.BlockSpec((slab, D_MODEL), lambda i: (i, 0))],
        out_specs=[row(D_MODEL), pl.BlockSpec((D_MODEL, tm), lambda i: (0, i)), st, st,
                   pl.BlockSpec((None, D_MODEL, slab), lambda i: (i // per_tile, 0, i % per_tile))],
        out_shape=[
            jax.ShapeDtypeStruct((T, D_MODEL), F32),
            jax.ShapeDtypeStruct((D_MODEL, T), BF16),
            jax.ShapeDtypeStruct((PEER_HEADS, N_KEYS, T), F32),
            jax.ShapeDtypeStruct((PEER_HEADS, N_KEYS, T), F32),
            jax.ShapeDtypeStruct((N_EXPERTS // peer_tile, D_MODEL, peer_tile), BF16),
        ],
        compiler_params=pltpu.CompilerParams(
            dimension_semantics=("arbitrary",), vmem_limit_bytes=VMEM_LIMIT_BYTES),
        name="out_proj",
    )(y, x2, w_out_bf, g2, w_query_bf, sub_keys_bf, expert_up)


def _top_desc(s, k):
    out = []
    for _ in range(k):
        mx = jnp.max(s, axis=0, keepdims=True)
        out.append(mx)
        s = jnp.where(s == mx, -jnp.inf, s)
    return out


def _batcher_network(n):
    pairs, p = [], 1
    while p < n:
        k = p
        while k >= 1:
            for j in range(k % p, n - k, 2 * k):
                for i in range(min(k, n - j - k)):
                    if (i + j) // (2 * p) == (i + j + k) // (2 * p):
                        pairs.append((i + j, i + j + k))
            k //= 2
        p *= 2
    return pairs


ROUTE_SUBLANES = 8
ROUTE_LISTS = N_KEYS // ROUTE_SUBLANES
ROUTE_NET = _batcher_network(ROUTE_LISTS)
ROUTE_RANKS = PEER_TOPK + 1


def _top_desc_keys(s, k):
    lists = [s[i * ROUTE_SUBLANES:(i + 1) * ROUTE_SUBLANES] for i in range(ROUTE_LISTS)]
    for a, b in ROUTE_NET:
        lists[a], lists[b] = jnp.maximum(lists[a], lists[b]), jnp.minimum(lists[a], lists[b])
    lists.append(jnp.full_like(lists[0], -jnp.inf))
    out = []
    for t in range(k):
        mx = jnp.max(lists[0], axis=0, keepdims=True)
        out.append(mx)
        if t + 1 < k:
            hit = lists[0] == mx
            depth = min(k - 1 - t, ROUTE_LISTS)
            for d in range(depth):
                lists[d] = jnp.where(hit, lists[d + 1], lists[d])
    return out


def _route_kernel(s1_ref, s2_ref, thr_ref, p1_ref, p2_ref):
    pairs = [(i, j) for i in range(ROUTE_RANKS) for j in range(ROUTE_RANKS)
             if (i + 1) * (j + 1) <= ROUTE_RANKS]
    for h in range(PEER_HEADS):
        s1 = s1_ref[h]
        s2 = s2_ref[h]
        v1 = _top_desc_keys(s1, ROUTE_RANKS)
        v2 = _top_desc_keys(s2, ROUTE_RANKS)
        v1p = jnp.concatenate([v1[i] for i, _ in pairs], axis=0)
        v2p = jnp.concatenate([v2[j] for _, j in pairs], axis=0)
        cand = v1p + v2p
        top = _top_desc(cand, ROUTE_RANKS)
        tau = 0.5 * (top[PEER_TOPK - 1] + top[PEER_TOPK])
        m1, m2 = v1[0], v2[0]
        sel = jnp.exp(v2p - m2) >= jnp.exp((tau - v1p) - m2)
        z = jnp.sum(jnp.where(sel, jnp.exp(cand - (m1 + m2)), 0.0), axis=0, keepdims=True)
        pthr = jnp.exp((tau - s1) - m2)
        p1 = 0.5 * (jnp.exp(s1 - m1) / z)
        for c in range(pthr.shape[1] // 128):
            thr_ref[c, h] = pthr[:, c * 128:(c + 1) * 128]
            p1_ref[c, h] = p1[:, c * 128:(c + 1) * 128]
        p2_ref[h] = jnp.exp(s2 - m2)


def _route(s1t, s2t, tr):
    T = s1t.shape[-1]
    st = pl.BlockSpec((PEER_HEADS, N_KEYS, tr), lambda i: (0, 0, i))
    full = jax.ShapeDtypeStruct((PEER_HEADS, N_KEYS, T), F32)
    chunked = jax.ShapeDtypeStruct((T // 128, PEER_HEADS, N_KEYS, 128), F32)
    ct = pl.BlockSpec((tr // 128, PEER_HEADS, N_KEYS, 128), lambda i: (i, 0, 0, 0))
    return pl.pallas_call(
        _route_kernel,
        grid=(T // tr,),
        in_specs=[st, st],
        out_specs=[ct, ct, st],
        out_shape=[chunked, chunked, full],
        compiler_params=pltpu.CompilerParams(
            dimension_semantics=("arbitrary",), vmem_limit_bytes=VMEM_LIMIT_BYTES),
        name="route",
    )(s1t, s2t)


PEER_SUBLANES = 8
PEER_MXU_COLS = 256
PEER_KEY_ROWS = 16
PEER_VMEM_LIMIT_BYTES = 61 * 1024 * 1024


def _peer_order(n_chunks):
    return [(stage, qc) for qc in range(n_chunks) for stage in (3, 2, 1)]


def _peer_kernel(h2t_ref, dn_ref, upt_ref, thr_ref, p1_ref, p2_ref, o_ref,
                 a_scr, c_scr, *, rows_e1, n_etiles, n_tiles):
    s = pl.program_id(0)
    tm = o_ref.shape[1]

    @pl.when(s == 0)
    def _():
        a_scr[...] = jnp.zeros_like(a_scr)
        c_scr[...] = jnp.zeros_like(c_scr)

    @pl.when((s == 0) | ((s - 2) % n_etiles == 0))
    def _():
        o_ref[...] = jnp.zeros_like(o_ref)

    t2 = jnp.clip(s - 1, 0, n_tiles - 1)
    row0 = ((t2 % n_etiles) * rows_e1) % PEER_SUBLANES

    def stage3(qc):
        qsl = slice(qc * PEER_MXU_COLS, (qc + 1) * PEER_MXU_COLS)
        o_ref[:, qsl] += jnp.dot(upt_ref[...], c_scr[:, qsl], preferred_element_type=F32)

    def stage2(qc):
        for lc in range(PEER_MXU_COLS // 128):
            lcg = qc * (PEER_MXU_COLS // 128) + lc
            lsl = slice(lcg * 128, (lcg + 1) * 128)
            for k0 in range(0, N_KEYS, PEER_KEY_ROWS):
                ksl = slice(k0, k0 + PEER_KEY_ROWS)
                w = [jnp.zeros((PEER_KEY_ROWS, 128), F32) for _ in range(rows_e1)]
                for h in range(PEER_HEADS):
                    p2 = p2_ref[h, ksl, lsl]
                    for r in range(rows_e1):
                        bcast = pl.ds(row0 + r, PEER_KEY_ROWS, stride=0)
                        w[r] = w[r] + jnp.where(p2 >= thr_ref[lcg, h, bcast, :],
                                                p1_ref[lcg, h, bcast, :] * p2, 0.0)
                for r in range(rows_e1):
                    esl = slice(r * N_KEYS + k0, r * N_KEYS + k0 + PEER_KEY_ROWS)
                    a = a_scr[esl, lsl]
                    c_scr[esl, lsl] = (w[r] * (a * (1.0 + lax.erf(a * np_sqrt_half)))).astype(BF16)

    def stage1(qc):
        qsl = slice(qc * PEER_MXU_COLS, (qc + 1) * PEER_MXU_COLS)
        a_scr[:, qsl] = jnp.dot(dn_ref[...], h2t_ref[:, qsl], preferred_element_type=F32)

    stages = {1: stage1, 2: stage2, 3: stage3}
    for stage, qc in _peer_order(tm // PEER_MXU_COLS):
        stages[stage](qc)


def _peer(h2t, dn_bf, upt_bf, thr, p1t, p2t, tm, rows_e1):
    T = h2t.shape[1]
    E = rows_e1 * N_KEYS
    n_etiles = N_EXPERTS // E
    n_tiles = (T // tm) * n_etiles
    assert tm % PEER_MXU_COLS == 0 and PEER_SUBLANES % rows_e1 == 0

    def tile(delay):
        def f(s):
            t = jnp.clip(s - delay, 0, n_tiles - 1)
            return t // n_etiles, t % n_etiles
        return f

    st1, st2, st3 = tile(0), tile(1), tile(2)
    rows_blk = pl.BlockSpec(
        (tm // 128, PEER_HEADS, PEER_SUBLANES, 128),
        lambda s: (st2(s)[0], 0, (st2(s)[1] * rows_e1) // PEER_SUBLANES, 0))
    return pl.pallas_call(
        functools.partial(_peer_kernel, rows_e1=rows_e1, n_etiles=n_etiles, n_tiles=n_tiles),
        grid=(n_tiles + 2,),
        in_specs=[
            pl.BlockSpec((D_MODEL, tm), lambda s: (0, st1(s)[0])),
            pl.BlockSpec((E, D_MODEL), lambda s: (st1(s)[1], 0)),
            pl.BlockSpec((None, D_MODEL, E), lambda s: (st3(s)[1], 0, 0)),
            rows_blk, rows_blk,
            pl.BlockSpec((PEER_HEADS, N_KEYS, tm), lambda s: (0, 0, st2(s)[0])),
        ],
        out_specs=pl.BlockSpec((D_MODEL, tm), lambda s: (0, st3(s)[0])),
        out_shape=jax.ShapeDtypeStruct((D_MODEL, T), F32),
        scratch_shapes=[pltpu.VMEM((E, tm), F32), pltpu.VMEM((E, tm), BF16)],
        compiler_params=pltpu.CompilerParams(
            dimension_semantics=("arbitrary",), vmem_limit_bytes=PEER_VMEM_LIMIT_BYTES),
        name="peer",
    )(h2t, dn_bf, upt_bf, thr, p1t, p2t)


def _final_kernel(x1_ref, pt_ref, g_ref, o_ref):
    xo = x1_ref[...] + pt_ref[...].T
    o_ref[...] = _rmsnorm(xo, g_ref[...])


def _final(x1, peer_t, gf, tm):
    T = x1.shape[0]
    return pl.pallas_call(
        _final_kernel,
        grid=(T // tm,),
        in_specs=[pl.BlockSpec((tm, D_MODEL), lambda i: (i, 0)),
                  pl.BlockSpec((D_MODEL, tm), lambda i: (0, i)),
                  _resident((1, D_MODEL))],
        out_specs=pl.BlockSpec((tm, D_MODEL), lambda i: (i, 0)),
        out_shape=jax.ShapeDtypeStruct((T, D_MODEL), F32),
        compiler_params=pltpu.CompilerParams(
            dimension_semantics=("arbitrary",), vmem_limit_bytes=VMEM_LIMIT_BYTES),
        name="final",
    )(x1, peer_t, gf)


def kernel(x, norm1_g, w_in, sgu_ln_g, sgu_ln_b, w_spatial, b_spatial, attn_sinks, w_out,
           norm2_g, w_query, sub_keys, expert_down, expert_up, norm_f_g):
    B, S, D = x.shape
    assert D == D_MODEL and S % WINDOW == 0 and norm1_g.shape[0] == 1
    T = B * S
    nb = S // WINDOW
    x2 = x.reshape(T, D)

    u, v, q, kv, dn_bf = _in_proj(
        x2, norm1_g[0].reshape(1, D), w_in[0].astype(BF16), expert_down[0], tm=IN_PROJ_ROWS)

    bs_full = jnp.repeat(b_spatial[0].T, SGU_DIM, axis=1)
    y, w_out_bf, w_query_bf = _mix(
        attn_sinks[0], u, v, q, kv,
        sgu_ln_g[0].reshape(1, SGU_WIDTH), sgu_ln_b[0].reshape(1, SGU_WIDTH),
        w_spatial[0], bs_full, w_out[0], w_query[0], B, nb)

    x1, h2t, s1t, s2t, upt_bf = _out_proj(
        y, x2, w_out_bf, norm2_g[0].reshape(1, D), w_query_bf,
        sub_keys[0].astype(BF16), expert_up[0], tm=OUT_PROJ_ROWS,
        peer_tile=PEER_ROWS_E1 * N_KEYS)

    pthr, p1t, p2t = _route(s1t, s2t, tr=ROUTE_TOKENS)

    peer_t = _peer(h2t, dn_bf, upt_bf, pthr, p1t, p2t, tm=PEER_TOKENS, rows_e1=PEER_ROWS_E1)

    out = _final(x1, peer_t, norm_f_g.reshape(1, D), tm=FINAL_ROWS)
    return out.reshape(B, S, D)
```

```python
import functools
import math

import jax
import jax.numpy as jnp
from jax import lax
from jax.experimental import pallas as pl
from jax.experimental.pallas import tpu as pltpu

D_MODEL = 2048
CHUNK = 128
SGU_GROUPS = 8
SGU_DIM = 128
SGU_WIDTH = SGU_GROUPS * SGU_DIM
HEAD_DIM = 64
N_Q_HEADS = 16
N_KV_HEADS = 2
Q_PER_KV = N_Q_HEADS // N_KV_HEADS
WINDOW = 128
ATTN_WIDTH = N_Q_HEADS * HEAD_DIM
KV_WIDTH = N_KV_HEADS * HEAD_DIM
IN_WIDTH = 2 * SGU_WIDTH + ATTN_WIDTH + 2 * KV_WIDTH
PEER_HEADS = 8
N_KEYS = 128
N_EXPERTS = N_KEYS * N_KEYS
PEER_TOPK = 16
D_QUERY = 256
EPS = 1e-6
ATTN_SCALE = HEAD_DIM ** -0.5

VMEM_LIMIT_BYTES = 56 * 1024 * 1024

IN_PROJ_ROWS = 512
OUT_PROJ_ROWS = 256
ROUTE_TOKENS = 256
PEER_TOKENS = 1024
PEER_ROWS_E1 = 8
FINAL_ROWS = 256

BF16 = jnp.bfloat16
F32 = jnp.float32


np_sqrt_half = math.sqrt(0.5)


def _gelu(x):
    return 0.5 * x * (1.0 + lax.erf(x * np_sqrt_half))


def _rmsnorm(x, g):
    return x * lax.rsqrt(jnp.mean(x * x, axis=-1, keepdims=True) + EPS) * g


def _resident(shape):
    nd = len(shape)
    return pl.BlockSpec(shape, lambda *_: (0,) * nd, pipeline_mode=pl.Buffered(1))


def _in_proj_kernel(x_ref, g_ref, w_ref, dn_ref, u_ref, v_ref, q_ref, kv_ref, dnb_ref):
    dnb_ref[...] = dn_ref[...].astype(BF16)
    h = _rmsnorm(x_ref[...], g_ref[...]).astype(BF16)
    o0, o1, o2, o3 = SGU_WIDTH, 2 * SGU_WIDTH, 2 * SGU_WIDTH + ATTN_WIDTH, IN_WIDTH
    zu = jnp.dot(h, w_ref[:, 0:o0], preferred_element_type=F32)
    u_ref[...] = _gelu(zu).astype(BF16)
    zv = jnp.dot(h, w_ref[:, o0:o1], preferred_element_type=F32)
    v_ref[...] = _gelu(zv).astype(BF16)
    q_ref[...] = jnp.dot(h, w_ref[:, o1:o2], preferred_element_type=F32).astype(BF16)
    kv_ref[...] = jnp.dot(h, w_ref[:, o2:o3], preferred_element_type=F32).astype(BF16)


def _expert_slab(steps):
    assert N_EXPERTS % steps == 0 and (N_EXPERTS // steps) % 128 == 0
    return N_EXPERTS // steps


def _in_proj(x2, g1, w_in_bf, expert_down, tm):
    T = x2.shape[0]
    steps = T // tm
    row = lambda w: pl.BlockSpec((tm, w), lambda i: (i, 0))
    slab_spec = pl.BlockSpec((_expert_slab(steps), D_MODEL), lambda i: (i, 0))
    return pl.pallas_call(
        _in_proj_kernel,
        grid=(steps,),
        in_specs=[row(D_MODEL), _resident((1, D_MODEL)), _resident((D_MODEL, IN_WIDTH)),
                  slab_spec],
        out_specs=[row(SGU_WIDTH), row(SGU_WIDTH), row(ATTN_WIDTH), row(2 * KV_WIDTH),
                   slab_spec],
        out_shape=[
            jax.ShapeDtypeStruct((T, SGU_WIDTH), BF16),
            jax.ShapeDtypeStruct((T, SGU_WIDTH), BF16),
            jax.ShapeDtypeStruct((T, ATTN_WIDTH), BF16),
            jax.ShapeDtypeStruct((T, 2 * KV_WIDTH), BF16),
            jax.ShapeDtypeStruct((N_EXPERTS, D_MODEL), BF16),
        ],
        compiler_params=pltpu.CompilerParams(
            dimension_semantics=("arbitrary",), vmem_limit_bytes=VMEM_LIMIT_BYTES),
        name="in_proj",
    )(x2, g1, w_in_bf, expert_down)


def _mix_kernel(sink_ref, u_ref, v_ref, q_ref, kv_ref, kvp_ref, lng_ref, lnb_ref,
                ws_ref, bs_ref, wo_ref, wq_ref, y_ref, wob_ref, wqb_ref):
    n = pl.program_id(1)
    wob_ref[...] = wo_ref[...].astype(BF16)
    wqb_ref[...] = wq_ref[...].astype(BF16)
    row_t = lax.broadcasted_iota(jnp.int32, (CHUNK, CHUNK), 0)
    col_s = lax.broadcasted_iota(jnp.int32, (CHUNK, CHUNK), 1)
    causal = col_s <= row_t
    for g in range(SGU_GROUPS):
        sl = slice(g * SGU_DIM, (g + 1) * SGU_DIM)
        vg = v_ref[:, sl].astype(F32)
        mu = jnp.mean(vg, axis=-1, keepdims=True)
        var = jnp.mean(jnp.square(vg - mu), axis=-1, keepdims=True)
        vn = ((vg - mu) * lax.rsqrt(var + EPS) * lng_ref[:, sl] + lnb_ref[:, sl]).astype(BF16)
        w = jnp.where(causal, ws_ref[g], 0.0).astype(BF16)
        s = jnp.dot(w, vn, preferred_element_type=F32) + bs_ref[:, sl]
        y_ref[:, sl] = (u_ref[:, sl].astype(F32) * s).astype(BF16)

    qi = lax.broadcasted_iota(jnp.int32, (WINDOW, 2 * WINDOW), 0)
    kj = lax.broadcasted_iota(jnp.int32, (WINDOW, 2 * WINDOW), 1)
    band = (kj >= qi + 1) & (kj <= qi + WINDOW)
    mask = band & ((n > 0) | (kj >= WINDOW))
    for j in range(N_KV_HEADS):
        ksl = slice(j * HEAD_DIM, (j + 1) * HEAD_DIM)
        vsl = slice(KV_WIDTH + j * HEAD_DIM, KV_WIDTH + (j + 1) * HEAD_DIM)
        kk = jnp.concatenate([kvp_ref[:, ksl], kv_ref[:, ksl]], axis=0)
        vv = jnp.concatenate([kvp_ref[:, vsl], kv_ref[:, vsl]], axis=0)
        for gq in range(Q_PER_KV):
            hq = j * Q_PER_KV + gq
            qh = q_ref[:, hq * HEAD_DIM:(hq + 1) * HEAD_DIM]
            sc = lax.dot_general(qh, kk, (((1,), (1,)), ((), ())),
                                 preferred_element_type=F32) * ATTN_SCALE
            sc = jnp.where(mask, sc, -jnp.inf)
            sink = sink_ref[hq]
            m = jnp.maximum(jnp.max(sc, axis=-1, keepdims=True), sink)
            p = jnp.exp(sc - m)
            denom = jnp.sum(p, axis=-1, keepdims=True) + jnp.exp(sink - m)
            probs = (p / denom).astype(BF16)
            oh = jnp.dot(probs, vv, preferred_element_type=F32)
            c0 = SGU_WIDTH + hq * HEAD_DIM
            y_ref[:, c0:c0 + HEAD_DIM] = oh.astype(BF16)


def _mix(sinks, u, v, q, kv, lng, lnb, ws, bs_full, w_out, w_query, batch, nb):
    T = u.shape[0]
    steps = batch * nb
    assert w_out.shape == w_query.shape == (D_MODEL, D_MODEL)
    assert D_MODEL % steps == 0 and (D_MODEL // steps) % 16 == 0
    wrows = D_MODEL // steps
    cur = lambda w: pl.BlockSpec((WINDOW, w), lambda b, n: (b * nb + n, 0))
    prev = pl.BlockSpec((WINDOW, 2 * KV_WIDTH),
                        lambda b, n: (b * nb + jnp.maximum(n - 1, 0), 0))
    wslab = pl.BlockSpec((wrows, D_MODEL), lambda b, n: (b * nb + n, 0))
    return pl.pallas_call(
        _mix_kernel,
        grid=(batch, nb),
        in_specs=[
            pl.BlockSpec(memory_space=pltpu.SMEM),
            cur(SGU_WIDTH), cur(SGU_WIDTH), cur(ATTN_WIDTH), cur(2 * KV_WIDTH), prev,
            _resident((1, SGU_WIDTH)), _resident((1, SGU_WIDTH)),
            _resident((SGU_GROUPS, CHUNK, CHUNK)), _resident((CHUNK, SGU_WIDTH)),
            wslab, wslab,
        ],
        out_specs=[cur(D_MODEL), wslab, wslab],
        out_shape=[jax.ShapeDtypeStruct((T, D_MODEL), BF16),
                   jax.ShapeDtypeStruct((D_MODEL, D_MODEL), BF16),
                   jax.ShapeDtypeStruct((D_MODEL, PEER_HEADS * D_QUERY), BF16)],
        compiler_params=pltpu.CompilerParams(
            dimension_semantics=("arbitrary", "arbitrary"), vmem_limit_bytes=VMEM_LIMIT_BYTES),
        name="mix",
    )(sinks, u, v, q, kv, kv, lng, lnb, ws, bs_full, w_out, w_query)


def _out_proj_kernel(y_ref, x_ref, wo_ref, g2_ref, wq_ref, sk_ref, up_ref,
                     x1_ref, h2t_ref, s1_ref, s2_ref, upt_ref):
    upt_ref[...] = up_ref[...].astype(BF16).T
    x1 = x_ref[...] + jnp.dot(y_ref[...], wo_ref[...], preferred_element_type=F32)
    x1_ref[...] = x1
    h2 = _rmsnorm(x1, g2_ref[...]).astype(BF16)
    h2t_ref[...] = h2.T
    qry = jnp.dot(h2, wq_ref[...], preferred_element_type=F32).astype(BF16)
    half = D_QUERY // 2
    for h in range(PEER_HEADS):
        for c, s_ref in ((0, s1_ref), (1, s2_ref)):
            qc = qry[:, h * D_QUERY + c * half: h * D_QUERY + (c + 1) * half]
            s_ref[h] = lax.dot_general(sk_ref[c], qc, (((1,), (1,)), ((), ())),
                                       preferred_element_type=F32)


def _out_proj(y, x2, w_out_bf, g2, w_query_bf, sub_keys_bf, expert_up, tm, peer_tile):
    T = x2.shape[0]
    steps = T // tm
    slab = _expert_slab(steps)
    assert peer_tile % slab == 0
    per_tile = peer_tile // slab
    row = lambda w: pl.BlockSpec((tm, w), lambda i: (i, 0))
    st = pl.BlockSpec((PEER_HEADS, N_KEYS, tm), lambda i: (0, 0, i))
    return pl.pallas_call(
        _out_proj_kernel,
        grid=(steps,),
        in_specs=[row(D_MODEL), row(D_MODEL), _resident((D_MODEL, D_MODEL)),
                  _resident((1, D_MODEL)), _resident((D_MODEL, PEER_HEADS * D_QUERY)),
                  _resident((2, N_KEYS, D_QUERY // 2)),
                  pl.BlockSpec((slab, D_MODEL), lambda i: (i, 0))],
        out_specs=[row(D_MODEL), pl.BlockSpec((D_MODEL, tm), lambda i: (0, i)), st, st,
                   pl.BlockSpec((None, D_MODEL, slab), lambda i: (i // per_tile, 0, i % per_tile))],
        out_shape=[
            jax.ShapeDtypeStruct((T, D_MODEL), F32),
            jax.ShapeDtypeStruct((D_MODEL, T), BF16),
            jax.ShapeDtypeStruct((PEER_HEADS, N_KEYS, T), F32),
            jax.ShapeDtypeStruct((PEER_HEADS, N_KEYS, T), F32),
            jax.ShapeDtypeStruct((N_EXPERTS // peer_tile, D_MODEL, peer_tile), BF16),
        ],
        compiler_params=pltpu.CompilerParams(
            dimension_semantics=("arbitrary",), vmem_limit_bytes=VMEM_LIMIT_BYTES),
        name="out_proj",
    )(y, x2, w_out_bf, g2, w_query_bf, sub_keys_bf, expert_up)


def _top_desc(s, k):
    out = []
    for _ in range(k):
        mx = jnp.max(s, axis=0, keepdims=True)
        out.append(mx)
        s = jnp.where(s == mx, -jnp.inf, s)
    return out


def _batcher_network(n):
    pairs, p = [], 1
    while p < n:
        k = p
        while k >= 1:
            for j in range(k % p, n - k, 2 * k):
                for i in range(min(k, n - j - k)):
                    if (i + j) // (2 * p) == (i + j + k) // (2 * p):
                        pairs.append((i + j, i + j + k))
            k //= 2
        p *= 2
    return pairs


ROUTE_SUBLANES = 8
ROUTE_LISTS = N_KEYS // ROUTE_SUBLANES
ROUTE_NET = _batcher_network(ROUTE_LISTS)
ROUTE_RANKS = PEER_TOPK + 1


def _top_desc_keys(s, k):
    lists = [s[i * ROUTE_SUBLANES:(i + 1) * ROUTE_SUBLANES] for i in range(ROUTE_LISTS)]
    for a, b in ROUTE_NET:
        lists[a], lists[b] = jnp.maximum(lists[a], lists[b]), jnp.minimum(lists[a], lists[b])
    lists.append(jnp.full_like(lists[0], -jnp.inf))
    out = []
    for t in range(k):
        mx = jnp.max(lists[0], axis=0, keepdims=True)
        out.append(mx)
        if t + 1 < k:
            hit = lists[0] == mx
            depth = min(k - 1 - t, ROUTE_LISTS)
            for d in range(depth):
                lists[d] = jnp.where(hit, lists[d + 1], lists[d])
    return out


def _route_kernel(s1_ref, s2_ref, thr_ref, p1_ref, p2_ref):
    pairs = [(i, j) for i in range(ROUTE_RANKS) for j in range(ROUTE_RANKS)
             if (i + 1) * (j + 1) <= ROUTE_RANKS]
    for h in range(PEER_HEADS):
        s1 = s1_ref[h]
        s2 = s2_ref[h]
        v1 = _top_desc_keys(s1, ROUTE_RANKS)
        v2 = _top_desc_keys(s2, ROUTE_RANKS)
        v1p = jnp.concatenate([v1[i] for i, _ in pairs], axis=0)
        v2p = jnp.concatenate([v2[j] for _, j in pairs], axis=0)
        cand = v1p + v2p
        top = _top_desc(cand, ROUTE_RANKS)
        tau = 0.5 * (top[PEER_TOPK - 1] + top[PEER_TOPK])
        m1, m2 = v1[0], v2[0]
        sel = jnp.exp(v2p - m2) >= jnp.exp((tau - v1p) - m2)
        z = jnp.sum(jnp.where(sel, jnp.exp(cand - (m1 + m2)), 0.0), axis=0, keepdims=True)
        pthr = jnp.exp((tau - s1) - m2)
        p1 = 0.5 * (jnp.exp(s1 - m1) / z)
        for c in range(pthr.shape[1] // 128):
            thr_ref[c, h] = pthr[:, c * 128:(c + 1) * 128]
            p1_ref[c, h] = p1[:, c * 128:(c + 1) * 128]
        p2_ref[h] = jnp.exp(s2 - m2)


def _route(s1t, s2t, tr):
    T = s1t.shape[-1]
    st = pl.BlockSpec((PEER_HEADS, N_KEYS, tr), lambda i: (0, 0, i))
    full = jax.ShapeDtypeStruct((PEER_HEADS, N_KEYS, T), F32)
    chunked = jax.ShapeDtypeStruct((T // 128, PEER_HEADS, N_KEYS, 128), F32)
    ct = pl.BlockSpec((tr // 128, PEER_HEADS, N_KEYS, 128), lambda i: (i, 0, 0, 0))
    return pl.pallas_call(
        _route_kernel,
        grid=(T // tr,),
        in_specs=[st, st],
        out_specs=[ct, ct, st],
        out_shape=[chunked, chunked, full],
        compiler_params=pltpu.CompilerParams(
            dimension_semantics=("arbitrary",), vmem_limit_bytes=VMEM_LIMIT_BYTES),
        name="route",
    )(s1t, s2t)


PEER_SUBLANES = 8
PEER_MXU_COLS = 256
PEER_KEY_ROWS = 16
PEER_VMEM_LIMIT_BYTES = 61 * 1024 * 1024


def _peer_order(n_chunks):
    return [(stage, qc) for qc in range(n_chunks) for stage in (3, 2, 1)]


def _peer_kernel(h2t_ref, dn_ref, upt_ref, thr_ref, p1_ref, p2_ref, o_ref,
                 a_scr, c_scr, *, rows_e1, n_etiles, n_tiles):
    s = pl.program_id(0)
    tm = o_ref.shape[1]

    @pl.when(s == 0)
    def _():
        a_scr[...] = jnp.zeros_like(a_scr)
        c_scr[...] = jnp.zeros_like(c_scr)

    @pl.when((s == 0) | ((s - 2) % n_etiles == 0))
    def _():
        o_ref[...] = jnp.zeros_like(o_ref)

    t2 = jnp.clip(s - 1, 0, n_tiles - 1)
    row0 = ((t2 % n_etiles) * rows_e1) % PEER_SUBLANES

    def stage3(qc):
        qsl = slice(qc * PEER_MXU_COLS, (qc + 1) * PEER_MXU_COLS)
        o_ref[:, qsl] += jnp.dot(upt_ref[...], c_scr[:, qsl], preferred_element_type=F32)

    def stage2(qc):
        for lc in range(PEER_MXU_COLS // 128):
            lcg = qc * (PEER_MXU_COLS // 128) + lc
            lsl = slice(lcg * 128, (lcg + 1) * 128)
            for k0 in range(0, N_KEYS, PEER_KEY_ROWS):
                ksl = slice(k0, k0 + PEER_KEY_ROWS)
                w = [jnp.zeros((PEER_KEY_ROWS, 128), F32) for _ in range(rows_e1)]
                for h in range(PEER_HEADS):
                    p2 = p2_ref[h, ksl, lsl]
                    for r in range(rows_e1):
                        bcast = pl.ds(row0 + r, PEER_KEY_ROWS, stride=0)
                        w[r] = w[r] + jnp.where(p2 >= thr_ref[lcg, h, bcast, :],
                                                p1_ref[lcg, h, bcast, :] * p2, 0.0)
                for r in range(rows_e1):
                    esl = slice(r * N_KEYS + k0, r * N_KEYS + k0 + PEER_KEY_ROWS)
                    c_scr[esl, lsl] = (w[r] * a_scr[esl, lsl].astype(F32)).astype(BF16)

    def stage1(qc):
        qsl = slice(qc * PEER_MXU_COLS, (qc + 1) * PEER_MXU_COLS)
        a = jnp.dot(dn_ref[...], h2t_ref[:, qsl], preferred_element_type=F32)
        a_scr[:, qsl] = (a * (1.0 + lax.erf(a * np_sqrt_half))).astype(BF16)

    stages = {1: stage1, 2: stage2, 3: stage3}
    for stage, qc in _peer_order(tm // PEER_MXU_COLS):
        stages[stage](qc)


def _peer(h2t, dn_bf, upt_bf, thr, p1t, p2t, tm, rows_e1):
    T = h2t.shape[1]
    E = rows_e1 * N_KEYS
    n_etiles = N_EXPERTS // E
    n_tiles = (T // tm) * n_etiles
    assert tm % PEER_MXU_COLS == 0 and PEER_SUBLANES % rows_e1 == 0

    def tile(delay):
        def f(s):
            t = jnp.clip(s - delay, 0, n_tiles - 1)
            return t // n_etiles, t % n_etiles
        return f

    st1, st2, st3 = tile(0), tile(1), tile(2)
    rows_blk = pl.BlockSpec(
        (tm // 128, PEER_HEADS, PEER_SUBLANES, 128),
        lambda s: (st2(s)[0], 0, (st2(s)[1] * rows_e1) // PEER_SUBLANES, 0))
    return pl.pallas_call(
        functools.partial(_peer_kernel, rows_e1=rows_e1, n_etiles=n_etiles, n_tiles=n_tiles),
        grid=(n_tiles + 2,),
        in_specs=[
            pl.BlockSpec((D_MODEL, tm), lambda s: (0, st1(s)[0])),
            pl.BlockSpec((E, D_MODEL), lambda s: (st1(s)[1], 0)),
            pl.BlockSpec((None, D_MODEL, E), lambda s: (st3(s)[1], 0, 0)),
            rows_blk, rows_blk,
            pl.BlockSpec((PEER_HEADS, N_KEYS, tm), lambda s: (0, 0, st2(s)[0])),
        ],
        out_specs=pl.BlockSpec((D_MODEL, tm), lambda s: (0, st3(s)[0])),
        out_shape=jax.ShapeDtypeStruct((D_MODEL, T), F32),
        scratch_shapes=[pltpu.VMEM((E, tm), BF16), pltpu.VMEM((E, tm), BF16)],
        compiler_params=pltpu.CompilerParams(
            dimension_semantics=("arbitrary",), vmem_limit_bytes=PEER_VMEM_LIMIT_BYTES),
        name="peer",
    )(h2t, dn_bf, upt_bf, thr, p1t, p2t)


def _final_kernel(x1_ref, pt_ref, g_ref, o_ref):
    xo = x1_ref[...] + pt_ref[...].T
    o_ref[...] = _rmsnorm(xo, g_ref[...])


def _final(x1, peer_t, gf, tm):
    T = x1.shape[0]
    return pl.pallas_call(
        _final_kernel,
        grid=(T // tm,),
        in_specs=[pl.BlockSpec((tm, D_MODEL), lambda i: (i, 0)),
                  pl.BlockSpec((D_MODEL, tm), lambda i: (0, i)),
                  _resident((1, D_MODEL))],
        out_specs=pl.BlockSpec((tm, D_MODEL), lambda i: (i, 0)),
        out_shape=jax.ShapeDtypeStruct((T, D_MODEL), F32),
        compiler_params=pltpu.CompilerParams(
            dimension_semantics=("arbitrary",), vmem_limit_bytes=VMEM_LIMIT_BYTES),
        name="final",
    )(x1, peer_t, gf)


def kernel(x, norm1_g, w_in, sgu_ln_g, sgu_ln_b, w_spatial, b_spatial, attn_sinks, w_out,
           norm2_g, w_query, sub_keys, expert_down, expert_up, norm_f_g):
    B, S, D = x.shape
    assert D == D_MODEL and S % WINDOW == 0 and norm1_g.shape[0] == 1
    T = B * S
    nb = S // WINDOW
    x2 = x.reshape(T, D)

    u, v, q, kv, dn_bf = _in_proj(
        x2, norm1_g[0].reshape(1, D), w_in[0].astype(BF16), expert_down[0], tm=IN_PROJ_ROWS)

    bs_full = jnp.repeat(b_spatial[0].T, SGU_DIM, axis=1)
    y, w_out_bf, w_query_bf = _mix(
        attn_sinks[0], u, v, q, kv,
        sgu_ln_g[0].reshape(1, SGU_WIDTH), sgu_ln_b[0].reshape(1, SGU_WIDTH),
        w_spatial[0], bs_full, w_out[0], w_query[0], B, nb)

    x1, h2t, s1t, s2t, upt_bf = _out_proj(
        y, x2, w_out_bf, norm2_g[0].reshape(1, D), w_query_bf,
        sub_keys[0].astype(BF16), expert_up[0], tm=OUT_PROJ_ROWS,
        peer_tile=PEER_ROWS_E1 * N_KEYS)

    pthr, p1t, p2t = _route(s1t, s2t, tr=ROUTE_TOKENS)

    peer_t = _peer(h2t, dn_bf, upt_bf, pthr, p1t, p2t, tm=PEER_TOKENS, rows_e1=PEER_ROWS_E1)

    out = _final(x1, peer_t, norm_f_g.reshape(1, D), tm=FINAL_ROWS)
    return out.reshape(B, S, D)
```
